```python
import jax, jax.numpy as jnp
from jax import lax
import numpy as np

D_MODEL = 1024
BATCH = 8
SEQ = 4096
DEPTH = 4

GRID_W = 64
N_MEM = 256
M_HEADS = 4
M_DQK = 64
M_DV = 128
M_CHUNK = 128
M_CONV = 5
NA_HEADS = 4
NA_DH = 64
NA_KH_MAX = 8
NA_KW = 16
MEM_HEADS = 4
MEM_DH = 64
N_EXPERTS = 16
EC_CAPACITY = 2
D_FF_EXPERT = 2048

N_GATES = 4 * M_HEADS
MIX_WIDTH = M_HEADS * M_DV + NA_HEADS * NA_DH + MEM_HEADS * MEM_DH
COL_SIZES = (M_HEADS * M_DQK, M_HEADS * M_DQK, M_HEADS * M_DV, M_HEADS * M_DV, N_GATES,
             NA_HEADS * NA_DH, NA_HEADS * NA_DH, NA_HEADS * NA_DH, MEM_HEADS * MEM_DH)
D_IN_PROJ = sum(COL_SIZES)
EPS = 1e-6

kernel_name = 'hybrid_mlstm_natten_ec_moe_encoder'


def rms_norm(x, g):
    xf = x.astype(jnp.float32)
    y = xf * lax.rsqrt(jnp.mean(xf * xf, axis=-1, keepdims=True) + EPS)
    return (y * g.astype(jnp.float32)).astype(x.dtype)


def to_heads(t, n_heads):
    bsz, s, _ = t.shape
    return t.reshape(bsz, s, n_heads, -1).transpose(0, 2, 1, 3)


def from_heads(t):
    bsz, nh, s, d = t.shape
    return t.transpose(0, 2, 1, 3).reshape(bsz, s, nh * d)


def split_columns(proj):
    parts, start = [], 0
    for size in COL_SIZES:
        parts.append(proj[..., start:start + size])
        start += size
    return parts


def centred_dwconv(u, w):
    kw = w.shape[0]
    pad = kw // 2
    s = u.shape[1]
    up = jnp.pad(u, ((0, 0), (pad, pad), (0, 0)))
    return sum(up[:, j:j + s] * w[j] for j in range(kw))


def mlstm_chunkwise(q, k, v, log_i, log_f):
    bsz, nh, s, dk = q.shape
    dv = v.shape[-1]
    L = M_CHUNK
    nc = s // L
    q = q.reshape(bsz, nh, nc, L, dk) * (dk ** -0.5)
    k = k.reshape(bsz, nh, nc, L, dk)
    v = v.reshape(bsz, nh, nc, L, dv)
    li = log_i.reshape(bsz, nh, nc, L)
    b = jnp.cumsum(log_f.reshape(bsz, nh, nc, L), axis=-1)
    g = b[..., -1]
    earlier_or_same = jnp.tril(jnp.ones((L, L), dtype=bool))
    d_log = jnp.where(earlier_or_same, b[..., :, None] - b[..., None, :] + li[..., None, :], -jnp.inf)
    w_log = g[..., None] - b + li
    a = jnp.max(w_log, axis=-1)
    w = jnp.exp(w_log - a[..., None])
    kv_chunk = jnp.einsum('bhcsk,bhcsv->bhckv', k * w[..., None], v)
    n_chunk = jnp.einsum('bhcs,bhcsk->bhck', w, k)

    def step(carry, inp):
        c_st, n_st, m_st = carry
        kv_c, n_c, a_c, g_c = inp
        m_new = jnp.maximum(g_c + m_st, a_c)
        s_prev = jnp.exp(g_c + m_st - m_new)
        s_cur = jnp.exp(a_c - m_new)
        c_new = s_prev[..., None, None] * c_st + s_cur[..., None, None] * kv_c
        n_new = s_prev[..., None] * n_st + s_cur[..., None] * n_c
        return (c_new, n_new, m_new), (c_st, n_st, m_st)

    init = (jnp.zeros((bsz, nh, dk, dv), jnp.float32),
            jnp.zeros((bsz, nh, dk), jnp.float32),
            jnp.zeros((bsz, nh), jnp.float32))
    xs = (jnp.moveaxis(kv_chunk, 2, 0), jnp.moveaxis(n_chunk, 2, 0), jnp.moveaxis(a, 2, 0), jnp.moveaxis(g, 2, 0))
    _, (c_prev, n_prev, m_prev) = lax.scan(step, init, xs)
    c_prev = jnp.moveaxis(c_prev, 0, 2)
    n_prev = jnp.moveaxis(n_prev, 0, 2)
    m_prev = jnp.moveaxis(m_prev, 0, 2)
    inter_log = b + m_prev[..., None]
    m_t = jnp.maximum(inter_log, jnp.max(d_log, axis=-1))
    s_inter = jnp.exp(inter_log - m_t)
    scores = jnp.einsum('bhctk,bhcsk->bhcts', q, k) * jnp.exp(d_log - m_t[..., None])
    num = s_inter[..., None] * jnp.einsum('bhctk,bhckv->bhctv', q, c_prev) + jnp.einsum('bhcts,bhcsv->bhctv', scores, v)
    den = s_inter * jnp.einsum('bhctk,bhck->bhct', q, n_prev) + jnp.sum(scores, axis=-1)
    h = num / jnp.maximum(jnp.abs(den), jnp.exp(-m_t))[..., None]
    return h.reshape(bsz, nh, s, dv)


def mlstm_mixer(q, k, v, o, gate_pre, b_gate, conv_w, g_head):
    dtype = v.dtype
    bsz, s, _ = v.shape
    qk = jax.nn.silu(centred_dwconv(jnp.concatenate([q, k], axis=-1), conv_w))
    q, k = jnp.split(qk, 2, axis=-1)
    q = to_heads(q, M_HEADS).astype(jnp.float32)
    k = to_heads(k, M_HEADS).astype(jnp.float32)
    v = to_heads(v, M_HEADS).astype(jnp.float32)
    gates = (gate_pre + b_gate).astype(jnp.float32).reshape(bsz, s, 4, M_HEADS).transpose(2, 0, 3, 1)
    i_fw, f_fw, i_bw, f_bw = gates[0], gates[1], gates[2], gates[3]
    flip = lambda t: jnp.flip(t, axis=2)
    h_fw = mlstm_chunkwise(q, k, v, i_fw, jax.nn.log_sigmoid(f_fw))
    h_bw = flip(mlstm_chunkwise(flip(q), flip(k), flip(v), flip(i_bw), jax.nn.log_sigmoid(flip(f_bw))))
    h = rms_norm(h_fw + h_bw, g_head.reshape(M_HEADS, 1, M_DV))
    return (jax.nn.sigmoid(o.astype(jnp.float32)) * from_heads(h)).astype(dtype)


def neighbourhood_attention(q, k, v, gq, gk, rpb):
    bsz, s, _ = q.shape
    rows = s // GRID_W
    kh = min(NA_KH_MAX, rows)
    q = rms_norm(to_heads(q, NA_HEADS), gq)
    k = rms_norm(to_heads(k, NA_HEADS), gk)
    v = to_heads(v, NA_HEADS)
    qg = q.reshape(bsz, NA_HEADS, rows, GRID_W, NA_DH)
    kg = k.reshape(bsz, NA_HEADS, rows, GRID_W, NA_DH)
    vg = v.reshape(bsz, NA_HEADS, rows, GRID_W, NA_DH)
    r_idx = jnp.arange(rows)
    row_start = jnp.clip(r_idx - kh // 2, 0, rows - kh)
    c_idx = jnp.arange(GRID_W)
    col_start = jnp.clip(c_idx - NA_KW // 2, 0, GRID_W - NA_KW)
    col_win = col_start[:, None] + jnp.arange(NA_KW)[None, :]
    dc = col_win - c_idx[:, None]
    scale = NA_DH ** -0.5

    def row_block(args):
        q_r, r, rs = args
        k_r = lax.dynamic_slice_in_dim(kg, rs, kh, axis=2)
        v_r = lax.dynamic_slice_in_dim(vg, rs, kh, axis=2)
        k_w = k_r[:, :, :, col_win]
        v_w = v_r[:, :, :, col_win]
        dr = rs + jnp.arange(kh) - r
        bias = rpb[:, dr[None, :, None] + NA_KH_MAX - 1, dc[:, None, :] + NA_KW - 1]
        sc = jnp.einsum('bhwd,bhiwjd->bhwij', q_r, k_w).astype(jnp.float32) * scale + bias.astype(jnp.float32)
        p = jax.nn.softmax(sc.reshape(sc.shape[:3] + (kh * NA_KW,)), axis=-1).reshape(sc.shape).astype(v_w.dtype)
        return jnp.einsum('bhwij,bhiwjd->bhwd', p, v_w)

    out = lax.map(row_block, (jnp.moveaxis(qg, 2, 0), r_idx, row_start))
    out = jnp.moveaxis(out, 0, 2).reshape(bsz, NA_HEADS, s, NA_DH)
    return from_heads(out)


def memory_cross_attention(q, mem_h, w_kv, gq, gk):
    q = rms_norm(to_heads(q, MEM_HEADS), gq)
    kv = jnp.einsum('bmd,de->bme', mem_h, w_kv)
    k_m, v_m = jnp.split(kv, 2, axis=-1)
    k_m = rms_norm(to_heads(k_m, MEM_HEADS), gk)
    v_m = to_heads(v_m, MEM_HEADS)
    sc = jnp.einsum('bhqd,bhmd->bhqm', q, k_m).astype(jnp.float32) * (MEM_DH ** -0.5)
    p = jax.nn.softmax(sc, axis=-1).astype(v_m.dtype)
    return from_heads(jnp.einsum('bhqm,bhmd->bhqd', p, v_m))


def expert_choice_ffn(h, w_router, w1, w3, w2):
    bsz, s, d = h.shape
    cap = EC_CAPACITY * s // N_EXPERTS
    aff = jax.nn.softmax(jnp.einsum('bsd,de->bse', h, w_router).astype(jnp.float32), axis=-1)
    gate, idx = lax.top_k(jnp.swapaxes(aff, 1, 2), cap)
    xe = jax.vmap(lambda hb, ib: hb[ib])(h, idx)
    hid = jax.nn.silu(jnp.einsum('becd,edf->becf', xe, w1)) * jnp.einsum('becd,edf->becf', xe, w3)
    ye = jnp.einsum('becf,efd->becd', hid, w2) * gate[..., None].astype(h.dtype)
    return jax.vmap(lambda yb, ib: jnp.zeros((s, d), h.dtype).at[ib.reshape(-1)].add(yb.reshape(-1, d)))(ye, idx)


def setup_inputs(seed: int = 0) -> dict:
    key = jax.random.key(seed)
    ks = jax.random.split(key, 24)
    nrm = lambda k, shape, sc: jax.random.normal(k, shape, jnp.float32) * sc
    gain = lambda k, shape: 1.0 + 0.02 * jax.random.normal(k, shape, jnp.float32)
    f_bias = jnp.linspace(3.0, 6.0, M_HEADS)
    zeros_h = jnp.zeros((M_HEADS,), jnp.float32)
    gate_base = jnp.concatenate([zeros_h, f_bias, zeros_h, f_bias])
    return {
        'x': nrm(ks[0], (BATCH, SEQ, D_MODEL), 1.0),
        'mem': nrm(ks[1], (BATCH, N_MEM, D_MODEL), 1.0),
        'g_mix': gain(ks[2], (DEPTH, D_MODEL)),
        'w_in': nrm(ks[3], (DEPTH, D_MODEL, D_IN_PROJ), D_MODEL ** -0.5),
        'b_gates': gate_base[None, :] + nrm(ks[4], (DEPTH, N_GATES), 0.1),
        'conv_qk': nrm(ks[5], (DEPTH, M_CONV, 2 * M_HEADS * M_DQK), M_CONV ** -0.5),
        'g_mlstm_head': gain(ks[6], (DEPTH, M_HEADS * M_DV)),
        'na_gq': gain(ks[7], (DEPTH, NA_DH)),
        'na_gk': gain(ks[8], (DEPTH, NA_DH)),
        'na_rpb': nrm(ks[9], (DEPTH, NA_HEADS, 2 * NA_KH_MAX - 1, 2 * NA_KW - 1), 0.1),
        'g_mem': gain(ks[10], (DEPTH, D_MODEL)),
        'w_mem_kv': nrm(ks[11], (DEPTH, D_MODEL, 2 * MEM_HEADS * MEM_DH), D_MODEL ** -0.5),
        'mem_gq': gain(ks[12], (DEPTH, MEM_DH)),
        'mem_gk': gain(ks[13], (DEPTH, MEM_DH)),
        'w_out': nrm(ks[14], (DEPTH, MIX_WIDTH, D_MODEL), MIX_WIDTH ** -0.5),
        'g_ffn': gain(ks[15], (DEPTH, D_MODEL)),
        'w_router': nrm(ks[16], (DEPTH, D_MODEL, N_EXPERTS), D_MODEL ** -0.5),
        'w1': nrm(ks[17], (DEPTH, N_EXPERTS, D_MODEL, D_FF_EXPERT), D_MODEL ** -0.5),
        'w3': nrm(ks[18], (DEPTH, N_EXPERTS, D_MODEL, D_FF_EXPERT), D_MODEL ** -0.5),
        'w2': nrm(ks[19], (DEPTH, N_EXPERTS, D_FF_EXPERT, D_MODEL), D_FF_EXPERT ** -0.5),
    }


def reference(x, mem, g_mix, w_in, b_gates, conv_qk, g_mlstm_head, na_gq, na_gk, na_rpb,
              g_mem, w_mem_kv, mem_gq, mem_gk, w_out, g_ffn, w_router, w1, w3, w2):
    for l in range(DEPTH):
        h = rms_norm(x, g_mix[l])
        proj = jnp.einsum('bsd,de->bse', h, w_in[l])
        mq, mk, mv, mo, mg, nq, nk, nv, cq = split_columns(proj)
        y_m = mlstm_mixer(mq, mk, mv, mo, mg, b_gates[l], conv_qk[l], g_mlstm_head[l])
        y_n = neighbourhood_attention(nq, nk, nv, na_gq[l], na_gk[l], na_rpb[l])
        y_c = memory_cross_attention(cq, rms_norm(mem, g_mem[l]), w_mem_kv[l], mem_gq[l], mem_gk[l])
        x = x + jnp.einsum('bse,ed->bsd', jnp.concatenate([y_m, y_n, y_c], axis=-1), w_out[l])
        x = x + expert_choice_ffn(rms_norm(x, g_ffn[l]), w_router[l], w1[l], w3[l], w2[l])
    return x
```

```python
import functools

import numpy as np
import jax
import jax.numpy as jnp
from jax import lax
from jax.experimental import pallas as pl
from jax.experimental.pallas import tpu as pltpu

F32 = jnp.float32
BF16 = jnp.bfloat16

D_MODEL = 1024
GRID_W = 64
M_HEADS = 4
M_DQK = 64
M_DV = 128
M_CHUNK = 128
M_CONV = 5
NA_HEADS = 4
NA_DH = 64
NA_KH = 8
NA_KW = 16
MEM_HEADS = 4
MEM_DH = 64
N_EXPERTS = 16
EC_CAPACITY = 2
D_FF = 2048
N_GATES = 4 * M_HEADS
EPS = 1e-6

LANES = 128
QK_W = 2 * M_HEADS * M_DQK
V_W = M_HEADS * M_DV
NA_W = NA_HEADS * NA_DH
MEM_W = MEM_HEADS * MEM_DH
SEG_QK = (0, QK_W)
SEG_V = (SEG_QK[1], SEG_QK[1] + V_W)
SEG_O = (SEG_V[1], SEG_V[1] + V_W)
SEG_G = (SEG_O[1], SEG_O[1] + LANES)
SEG_NA = (SEG_G[1], SEG_G[1] + 3 * NA_W)
SEG_CQ = (SEG_NA[1], SEG_NA[1] + MEM_W)
PROJ_W = SEG_CQ[1]
Z_W = D_MODEL + LANES
NA_QROWS = 4
NA_KROWS = NA_QROWS + NA_KH
NEG_BIG = -1e30
VMEM_LIMIT = 56 * 1024 * 1024


def _cparams(n_axes):
    return pltpu.CompilerParams(dimension_semantics=("arbitrary",) * n_axes, vmem_limit_bytes=VMEM_LIMIT)


def _dot(a, b):
    return jnp.dot(a, b, preferred_element_type=F32)


def _dot_nt(a, b):
    return lax.dot_general(a, b, (((1,), (1,)), ((), ())), preferred_element_type=F32)


def _dot_exact_rhs(a, b_bf16):
    a0, a1, a2 = _split_bf16(a, 3)
    return _dot(a0, b_bf16) + _dot(a1, b_bf16) + _dot(a2, b_bf16)


def _split_bf16(w, terms):
    parts, r = [], w
    for _ in range(terms):
        p = r.astype(BF16)
        parts.append(p)
        r = r - p.astype(F32)
    return parts


def _head_mask(width, head_dim, h):
    lane = lax.broadcasted_iota(jnp.int32, (1, width), 1)
    return (lane // head_dim) == h


def _same_head_matrix(width, head_dim):
    r = lax.broadcasted_iota(jnp.int32, (width, width), 0) // head_dim
    c = lax.broadcasted_iota(jnp.int32, (width, width), 1) // head_dim
    return jnp.where(r == c, 1.0, 0.0).astype(BF16)


def _per_head_rms(x, gain, head_dim):
    ss = _dot_exact_rhs(x * x, _same_head_matrix(x.shape[1], head_dim))
    return x * lax.rsqrt(ss * (1.0 / head_dim) + EPS) * gain


def _rms_rows(x, gain):
    ms = jnp.mean(x * x, axis=-1, keepdims=True)
    return x * lax.rsqrt(ms + EPS) * gain


def _softmax_rows(s):
    m = jnp.max(s, axis=-1, keepdims=True)
    e = jnp.exp(s - m)
    return e * (1.0 / jnp.sum(e, axis=-1, keepdims=True))


def _inproj_kernel(x_ref, g_ref, w_ref, qk_ref, v_ref, o_ref, gt_ref, na_ref, cq_ref):
    h = _rms_rows(x_ref[...], g_ref[...]).astype(BF16)
    for seg, out in ((SEG_QK, qk_ref), (SEG_V, v_ref), (SEG_O, o_ref), (SEG_G, gt_ref), (SEG_NA, na_ref),
                     (SEG_CQ, cq_ref)):
        out[...] = _dot(h, w_ref[:, seg[0]:seg[1]])


def _inproj(x2, g, w, tm):
    n = x2.shape[0]
    widths = [s[1] - s[0] for s in (SEG_QK, SEG_V, SEG_O, SEG_G, SEG_NA, SEG_CQ)]
    return pl.pallas_call(
        _inproj_kernel,
        grid=(n // tm,),
        in_specs=[pl.BlockSpec((tm, D_MODEL), lambda i: (i, 0)),
                  pl.BlockSpec((1, D_MODEL), lambda i: (0, 0)),
                  pl.BlockSpec((D_MODEL, PROJ_W), lambda i: (0, 0))],
        out_specs=[pl.BlockSpec((tm, w_), lambda i: (i, 0)) for w_ in widths],
        out_shape=[jax.ShapeDtypeStruct((n, w_), F32) for w_ in widths],
        compiler_params=_cparams(1),
    )(x2, g, w)


HALO = 8


def _log_sigmoid(x):
    return -(jnp.maximum(-x, 0.0) + jnp.log(1.0 + jnp.exp(-jnp.abs(x))))


def _mlstm_direction(reverse, chunk, n_chunks, qk_ref, prev_ref, next_ref, v_ref, g_ref, cw_ref, bg_ref, h_ref,
                     c_ref, n_ref, m_ref):
    L = M_CHUNK
    prev_on = jnp.where(chunk > 0, 1.0, 0.0)
    next_on = jnp.where(chunk < n_chunks - 1, 1.0, 0.0)
    ext = jnp.concatenate([prev_ref[...] * prev_on, qk_ref[...], next_ref[...] * next_on], axis=0)
    pad = M_CONV // 2
    conv = ext[HALO - pad:HALO - pad + L] * cw_ref[0:1, :]
    for j in range(1, M_CONV):
        conv = conv + ext[HALO - pad + j:HALO - pad + j + L] * cw_ref[j:j + 1, :]
    qk = conv * jax.nn.sigmoid(conv)
    q = qk[:, :QK_W // 2] * (M_DQK ** -0.5)
    k = qk[:, QK_W // 2:]
    k_b = k.astype(BF16)

    gates = g_ref[...] + bg_ref[...]
    log_f = _log_sigmoid(gates)
    t_i = lax.broadcasted_iota(jnp.int32, (L, L), 0)
    s_i = lax.broadcasted_iota(jnp.int32, (L, L), 1)
    visible = (s_i >= t_i) if reverse else (s_i <= t_i)
    b_all = _masked_prefix_sum(visible, log_f)
    b_all_t = b_all.T
    gates_t = gates.T
    off = 2 * M_HEADS if reverse else 0
    last = 0 if reverse else L - 1

    for h in range(M_HEADS):
        hm = _head_mask(QK_W // 2, M_DQK, h)
        gi = off + h
        gf = off + M_HEADS + h
        b_col = b_all[:, gf:gf + 1]
        b_row = b_all_t[gf:gf + 1, :]
        li_col = gates[:, gi:gi + 1]
        li_row = gates_t[gi:gi + 1, :]
        g_tot = b_all[last:last + 1, gf:gf + 1]
        sidx = (4 if reverse else 0) + h
        m_prev = m_ref[sidx:sidx + 1, 0:1]
        c_prev = c_ref[sidx]
        n_prev = n_ref[sidx:sidx + 1, :]

        q_h = jnp.where(hm, q, 0.0)
        q_hb = q_h.astype(BF16)
        v_hb = v_ref[:, h * M_DV:(h + 1) * M_DV].astype(BF16)
        d_log = jnp.where(visible, b_col - b_row + li_row, -jnp.inf)
        inter = b_col + m_prev
        m_t = jnp.maximum(inter, jnp.max(d_log, axis=-1, keepdims=True))
        s_inter = jnp.exp(inter - m_t)
        scores = _dot_nt(q_hb, k_b) * jnp.exp(d_log - m_t)
        num = s_inter * _dot(q_hb, c_prev.astype(BF16)) + _dot(scores.astype(BF16), v_hb)
        den = s_inter * jnp.sum(q_h * n_prev, axis=-1, keepdims=True) + jnp.sum(scores, axis=-1, keepdims=True)
        h_ref[:, h * M_DV:(h + 1) * M_DV] = num / jnp.maximum(jnp.abs(den), jnp.exp(-m_t))

        w_log = g_tot - b_col + li_col
        a = jnp.max(w_log, axis=0, keepdims=True)
        kw = jnp.where(hm, k, 0.0) * jnp.exp(w_log - a)
        kv = _dot(kw.T.astype(BF16), v_hb)
        n_c = jnp.sum(kw, axis=0, keepdims=True)
        m_new = jnp.maximum(g_tot + m_prev, a)
        s_prev = jnp.exp(g_tot + m_prev - m_new)
        s_cur = jnp.exp(a - m_new)
        c_ref[sidx] = s_prev * c_prev + s_cur * kv
        n_ref[sidx:sidx + 1, :] = s_prev * n_prev + s_cur * n_c
        m_ref[sidx:sidx + 1, :] = jnp.broadcast_to(m_new, (1, LANES))


def _masked_prefix_sum(visible, x):
    tri = jnp.where(visible, 1.0, 0.0).astype(BF16)
    x0, x1, x2 = _split_bf16(x, 3)
    return _dot(tri, x0) + _dot(tri, x1) + _dot(tri, x2)


def _mlstm_kernel(qkf, pvf, nxf, vf, gf, qkb, pvb, nxb, vb, gb, cw_ref, bg_ref, hf_ref, hb_ref, c_ref, n_ref, m_ref):
    c = pl.program_id(1)
    n_chunks = pl.num_programs(1)

    @pl.when(c == 0)
    def _():
        c_ref[...] = jnp.zeros_like(c_ref)
        n_ref[...] = jnp.zeros_like(n_ref)
        m_ref[...] = jnp.zeros_like(m_ref)

    _mlstm_direction(False, c, n_chunks, qkf, pvf, nxf, vf, gf, cw_ref, bg_ref, hf_ref, c_ref, n_ref, m_ref)
    _mlstm_direction(True, n_chunks - 1 - c, n_chunks, qkb, pvb, nxb, vb, gb, cw_ref, bg_ref, hb_ref, c_ref, n_ref,
                     m_ref)


def _mlstm(qk, v, gt, conv_w, b_gate, bsz, seq):
    L = M_CHUNK
    nc = seq // L
    hb = L // HALO
    n_halo = seq // HALO
    qk3 = qk.reshape(bsz, seq, QK_W)
    v3 = v.reshape(bsz, seq, V_W)
    g3 = gt.reshape(bsz, seq, LANES)

    def specs(cidx):
        return [
            pl.BlockSpec((None, L, QK_W), lambda b, c: (b, cidx(c), 0)),
            pl.BlockSpec((None, HALO, QK_W), lambda b, c: (b, jnp.maximum(cidx(c) * hb - 1, 0), 0)),
            pl.BlockSpec((None, HALO, QK_W), lambda b, c: (b, jnp.minimum((cidx(c) + 1) * hb, n_halo - 1), 0)),
            pl.BlockSpec((None, L, V_W), lambda b, c: (b, cidx(c), 0)),
            pl.BlockSpec((None, L, LANES), lambda b, c: (b, cidx(c), 0)),
        ]

    fw = lambda c: c
    bw = lambda c: nc - 1 - c
    h_fw, h_bw = pl.pallas_call(
        _mlstm_kernel,
        grid=(bsz, nc),
        in_specs=specs(fw) + specs(bw) + [pl.BlockSpec((M_CONV, QK_W), lambda b, c: (0, 0)),
                                          pl.BlockSpec((1, LANES), lambda b, c: (0, 0))],
        out_specs=[pl.BlockSpec((None, L, V_W), lambda b, c: (b, c, 0)),
                   pl.BlockSpec((None, L, V_W), lambda b, c: (b, nc - 1 - c, 0))],
        out_shape=[jax.ShapeDtypeStruct((bsz, seq, V_W), F32)] * 2,
        scratch_shapes=[pltpu.VMEM((2 * M_HEADS, QK_W // 2, M_DV), F32),
                        pltpu.VMEM((2 * M_HEADS, QK_W // 2), F32),
                        pltpu.VMEM((2 * M_HEADS, LANES), F32)],
        compiler_params=_cparams(2),
    )(qk3, qk3, qk3, v3, g3, qk3, qk3, qk3, v3, g3, conv_w, b_gate)
    return h_fw.reshape(bsz * seq, V_W), h_bw.reshape(bsz * seq, V_W)


def _na_kernel(q_ref, k0, k1, k2, v0, v1, v2, bias_ref, gq_ref, gk_ref, o_ref):
    q = _per_head_rms(q_ref[...], gq_ref[...], NA_DH) * (NA_DH ** -0.5)
    k = jnp.concatenate([k0[...], k1[...], k2[...]], axis=0)
    k_b = _per_head_rms(k, gk_ref[...], NA_DH).astype(BF16)
    v = jnp.concatenate([v0[...], v1[...], v2[...]], axis=0)
    acc = jnp.zeros(o_ref.shape, F32)
    for h in range(NA_HEADS):
        hm = _head_mask(NA_W, NA_DH, h)
        s = _dot_nt(jnp.where(hm, q, 0.0).astype(BF16), k_b) + bias_ref[h]
        p = _softmax_rows(s).astype(BF16)
        acc = acc + _dot(p, jnp.where(hm, v, 0.0).astype(BF16))
    o_ref[...] = acc


def _na_bias_index_tables(rows):
    nj = rows // NA_QROWS
    qn = NA_QROWS * GRID_W
    kn = NA_KROWS * GRID_W
    dr_tab = np.zeros((3, qn, kn), np.int32)
    dc_tab = np.zeros((3, qn, kn), np.int32)
    ok_tab = np.zeros((3, qn, kn), bool)
    qi = np.arange(qn) // GRID_W
    qc = np.arange(qn) % GRID_W
    ki = np.arange(kn) // GRID_W
    kc = np.arange(kn) % GRID_W
    for pat, j in enumerate((0, 1, nj - 1)):
        ks = min(max(NA_QROWS * j - NA_KH // 2, 0), rows - NA_KROWS)
        r = NA_QROWS * j + qi
        rs = np.clip(r - NA_KH // 2, 0, rows - NA_KH)
        cs = np.clip(qc - NA_KW // 2, 0, GRID_W - NA_KW)
        krow = ks + ki
        row_ok = (krow[None, :] >= rs[:, None]) & (krow[None, :] < rs[:, None] + NA_KH)
        col_ok = (kc[None, :] >= cs[:, None]) & (kc[None, :] < cs[:, None] + NA_KW)
        ok = row_ok & col_ok
        dr = krow[None, :] - r[:, None] + NA_KH - 1
        dc = kc[None, :] - qc[:, None] + NA_KW - 1
        dr_tab[pat] = np.where(ok, dr, 0)
        dc_tab[pat] = np.where(ok, dc, 0)
        ok_tab[pat] = ok
    return dr_tab, dc_tab, ok_tab


def _na_bias_table(rpb, rows):
    dr, dc, ok = _na_bias_index_tables(rows)
    return jnp.where(ok[None], rpb[:, dr, dc], NEG_BIG)


def _neighbourhood_attention(na, bias, gq, gk, bsz, seq):
    rows = seq // GRID_W
    nj = rows // NA_QROWS
    qn = NA_QROWS * GRID_W
    kn = NA_KROWS * GRID_W
    na3 = na.reshape(bsz, seq, 3 * NA_W)
    kstart = lambda j: jnp.clip(j - 1, 0, nj - 3)
    pattern = lambda j: jnp.minimum(j, 1) + jnp.maximum(j - (nj - 2), 0)
    kv_specs = [pl.BlockSpec((None, qn, NA_W), functools.partial(lambda j, b, i, part: (b, kstart(j) + i, part),
                                                                 i=i, part=part))
                for part in (1, 2) for i in range(3)]
    out = pl.pallas_call(
        _na_kernel,
        grid=(nj, bsz),
        in_specs=[pl.BlockSpec((None, qn, NA_W), lambda j, b: (b, j, 0))] + kv_specs + [
            pl.BlockSpec((NA_HEADS, None, qn, kn), lambda j, b: (0, pattern(j), 0, 0)),
            pl.BlockSpec((1, NA_W), lambda j, b: (0, 0)),
            pl.BlockSpec((1, NA_W), lambda j, b: (0, 0))],
        out_specs=pl.BlockSpec((None, qn, NA_W), lambda j, b: (b, j, 0)),
        out_shape=jax.ShapeDtypeStruct((bsz, seq, NA_W), F32),
        compiler_params=_cparams(2),
    )(na3, na3, na3, na3, na3, na3, na3, bias, gq, gk)
    return out.reshape(bsz * seq, NA_W)


def _mem_kv_kernel(mem_ref, g_ref, w_ref, gk_ref, k_ref, v_ref):
    h = _rms_rows(mem_ref[...], g_ref[...]).astype(BF16)
    kv = _dot(h, w_ref[...])
    k_ref[...] = _per_head_rms(kv[:, :MEM_W], gk_ref[...], MEM_DH).astype(BF16)
    v_ref[...] = kv[:, MEM_W:].astype(BF16)


def _mem_kv(mem, g_mem, w_kv, gk):
    bsz, n_mem, _ = mem.shape
    return pl.pallas_call(
        _mem_kv_kernel,
        grid=(bsz,),
        in_specs=[pl.BlockSpec((None, n_mem, D_MODEL), lambda b: (b, 0, 0)),
                  pl.BlockSpec((1, D_MODEL), lambda b: (0, 0)),
                  pl.BlockSpec((D_MODEL, 2 * MEM_W), lambda b: (0, 0)),
                  pl.BlockSpec((1, MEM_W), lambda b: (0, 0))],
        out_specs=[pl.BlockSpec((None, n_mem, MEM_W), lambda b: (b, 0, 0))] * 2,
        out_shape=[jax.ShapeDtypeStruct((bsz, n_mem, MEM_W), BF16)] * 2,
        compiler_params=_cparams(1),
    )(mem, g_mem, w_kv, gk)


def _mem_attn_kernel(q_ref, k_ref, v_ref, gq_ref, o_ref):
    q = _per_head_rms(q_ref[...], gq_ref[...], MEM_DH) * (MEM_DH ** -0.5)
    k_b = k_ref[...]
    v_b = v_ref[...]
    acc = jnp.zeros(o_ref.shape, F32)
    for h in range(MEM_HEADS):
        hm = _head_mask(MEM_W, MEM_DH, h)
        p = _softmax_rows(_dot_nt(jnp.where(hm, q, 0.0).astype(BF16), k_b)).astype(BF16)
        acc = acc + _dot(p, jnp.where(hm, v_b, jnp.zeros_like(v_b)))
    o_ref[...] = acc


def _mem_attention(cq, k_m, v_m, gq, bsz, seq, tq):
    n_mem = k_m.shape[1]
    out = pl.pallas_call(
        _mem_attn_kernel,
        grid=(bsz, seq // tq),
        in_specs=[pl.BlockSpec((None, tq, MEM_W), lambda b, i: (b, i, 0)),
                  pl.BlockSpec((None, n_mem, MEM_W), lambda b, i: (b, 0, 0)),
                  pl.BlockSpec((None, n_mem, MEM_W), lambda b, i: (b, 0, 0)),
                  pl.BlockSpec((1, MEM_W), lambda b, i: (0, 0))],
        out_specs=pl.BlockSpec((None, tq, MEM_W), lambda b, i: (b, i, 0)),
        out_shape=jax.ShapeDtypeStruct((bsz, seq, MEM_W), F32),
        compiler_params=_cparams(2),
    )(cq.reshape(bsz, seq, MEM_W), k_m, v_m, gq)
    return out.reshape(bsz * seq, MEM_W)


def _outproj_kernel(hf_ref, hb_ref, o_ref, yn_ref, yc_ref, x_ref, w_ref, gh_ref, gf_ref, wr_ref, xmid_ref, z_ref,
                    afft_ref):
    hs = hf_ref[...] + hb_ref[...]
    acc = x_ref[...]
    for h in range(M_HEADS):
        sl = slice(h * M_DV, (h + 1) * M_DV)
        y = jax.nn.sigmoid(o_ref[:, sl]) * _rms_rows(hs[:, sl], gh_ref[:, sl])
        acc = acc + _dot(y.astype(BF16), w_ref[sl, :])
    acc = acc + _dot(yn_ref[...].astype(BF16), w_ref[V_W:V_W + NA_W, :])
    acc = acc + _dot(yc_ref[...].astype(BF16), w_ref[V_W + NA_W:, :])
    xmid_ref[...] = acc
    h2 = _rms_rows(acc, gf_ref[...])
    z_ref[:, :D_MODEL] = h2
    h2_hi, h2_lo = _split_bf16(h2, 2)
    part = _dot(h2_hi, wr_ref[...]) + _dot(h2_lo, wr_ref[...])
    logits = part + pltpu.roll(part, LANES - N_EXPERTS, 1) + pltpu.roll(part, LANES - 2 * N_EXPERTS, 1)
    lane = lax.broadcasted_iota(jnp.int32, logits.shape, 1)
    aff = _softmax_rows(jnp.where(lane < N_EXPERTS, logits, NEG_BIG))
    z_ref[:, D_MODEL:] = aff
    afft_ref[...] = aff.T[:N_EXPERTS, :]


def _outproj(h_fw, h_bw, o, y_n, y_c, x2, w_out, g_head, g_ffn, w_router_t, tm):
    n = x2.shape[0]
    row = lambda w_: pl.BlockSpec((tm, w_), lambda i: (i, 0))
    full = lambda a: pl.BlockSpec(a.shape, lambda i: (0, 0))
    return pl.pallas_call(
        _outproj_kernel,
        grid=(n // tm,),
        in_specs=[row(V_W), row(V_W), row(V_W), row(NA_W), row(MEM_W), row(D_MODEL), full(w_out), full(g_head),
                  full(g_ffn), full(w_router_t)],
        out_specs=[row(D_MODEL), row(Z_W), pl.BlockSpec((N_EXPERTS, tm), lambda i: (0, i))],
        out_shape=[jax.ShapeDtypeStruct((n, D_MODEL), F32), jax.ShapeDtypeStruct((n, Z_W), F32),
                   jax.ShapeDtypeStruct((N_EXPERTS, n), F32)],
        compiler_params=_cparams(1),
    )(h_fw, h_bw, o, y_n, y_c, x2, w_out, g_head, g_ffn, w_router_t)


def _lane_cumsum(mask_f, tri_b):
    n = mask_f.shape[1]
    run = jnp.zeros((mask_f.shape[0], 1), F32)
    parts = []
    for j in range(n // LANES):
        local = _dot(mask_f[:, j * LANES:(j + 1) * LANES].astype(BF16), tri_b) + run
        parts.append(local)
        run = local[:, LANES - 1:LANES]
    return jnp.concatenate(parts, axis=1)


def _select_kernel(aff_ref, idx_ref, csum_ref, *, cap):
    bits = pltpu.bitcast(aff_ref[...], jnp.int32)
    n_e, n_tok = bits.shape
    thr = jnp.zeros((n_e, 1), jnp.int32)
    for bit in range(30, -1, -1):
        cand = thr | (1 << bit)
        cnt = jnp.sum(jnp.where(bits >= cand, 1.0, 0.0), axis=1, keepdims=True)
        thr = jnp.where(cnt >= cap, cand, thr)
    s_i = lax.broadcasted_iota(jnp.int32, (LANES, LANES), 0)
    t_i = lax.broadcasted_iota(jnp.int32, (LANES, LANES), 1)
    tri_b = jnp.where(s_i <= t_i, 1.0, 0.0).astype(BF16)
    gt = bits > thr
    eq_f = jnp.where(bits == thr, 1.0, 0.0)
    need = cap - jnp.sum(jnp.where(gt, 1.0, 0.0), axis=1, keepdims=True)
    eq_rank = _lane_cumsum(eq_f, tri_b)
    sel_f = jnp.where(gt | ((eq_f > 0.5) & (eq_rank <= need)), 1.0, 0.0)
    csum = _lane_cumsum(sel_f, tri_b)
    for e in range(n_e):
        csum_ref[e] = csum[e:e + 1, :]
    slot = lax.broadcasted_iota(jnp.int32, (cap, 1), 0).astype(F32)
    lane = lax.broadcasted_iota(jnp.int32, (cap, LANES), 1)

    def per_expert(e, cols):
        acc = jnp.zeros((cap, LANES), F32)
        for j in range(n_tok // LANES):
            acc = acc + jnp.where(csum_ref[e, :, j * LANES:(j + 1) * LANES] <= slot, 1.0, 0.0)
        return jnp.where(lane == e, jnp.sum(acc, axis=1, keepdims=True), cols)

    cols = lax.fori_loop(0, n_e, per_expert, jnp.zeros((cap, LANES), F32))
    idx_ref[...] = cols.T[:n_e, :].astype(jnp.int32)


def _select(aff_t, bsz, seq, cap):
    return pl.pallas_call(
        functools.partial(_select_kernel, cap=cap),
        grid=(bsz,),
        in_specs=[pl.BlockSpec((N_EXPERTS, seq), lambda b: (0, b))],
        out_specs=pl.BlockSpec((None, N_EXPERTS, cap), lambda b: (b, 0, 0)),
        out_shape=jax.ShapeDtypeStruct((bsz, N_EXPERTS, cap), jnp.int32),
        scratch_shapes=[pltpu.VMEM((N_EXPERTS, 1, seq), F32)],
        compiler_params=_cparams(1),
    )(aff_t)


FF_CHUNK = 512


def _experts_kernel(idx_ref, z_hbm, x_hbm, w1_ref, w3_ref, w2_ref, out_hbm, zbuf, ybuf, sems, *, cap, seq):
    del x_hbm
    e = pl.program_id(0)
    base = pl.program_id(1) * seq

    def z_copy(i):
        return pltpu.make_async_copy(z_hbm.at[pl.ds(base + idx_ref[0, i], 1)], zbuf.at[pl.ds(i, 1)], sems.at[0])

    def y_in_copy(i):
        return pltpu.make_async_copy(out_hbm.at[pl.ds(base + idx_ref[0, i], 1)], ybuf.at[pl.ds(i, 1)], sems.at[1])

    def y_out_copy(i):
        return pltpu.make_async_copy(ybuf.at[pl.ds(i, 1)], out_hbm.at[pl.ds(base + idx_ref[0, i], 1)], sems.at[2])

    def start_in(i, carry):
        z_copy(i).start()
        y_in_copy(i).start()
        return carry

    def wait_in(i, carry):
        z_copy(i).wait()
        y_in_copy(i).wait()
        return carry

    lax.fori_loop(0, cap, start_in, 0)
    lax.fori_loop(0, cap, wait_in, 0)

    xb = zbuf[:, :D_MODEL].astype(BF16)
    lane = lax.broadcasted_iota(jnp.int32, (cap, LANES), 1)
    gate = jnp.sum(jnp.where(lane == e, zbuf[:, D_MODEL:], 0.0), axis=1, keepdims=True)
    ye = jnp.zeros((cap, D_MODEL), F32)
    for f in range(D_FF // FF_CHUNK):
        sl = slice(f * FF_CHUNK, (f + 1) * FF_CHUNK)
        h1 = _dot(xb, w1_ref[:, sl])
        h3 = _dot(xb, w3_ref[:, sl])
        hid = (h1 * jax.nn.sigmoid(h1) * h3).astype(BF16)
        ye = ye + _dot(hid, w2_ref[sl, :])
    ybuf[...] = ybuf[...] + ye * gate

    def start_out(i, carry):
        y_out_copy(i).start()
        return carry

    def wait_out(i, carry):
        y_out_copy(i).wait()
        return carry

    lax.fori_loop(0, cap, start_out, 0)
    lax.fori_loop(0, cap, wait_out, 0)


def _experts(idx, z, x_mid, w1, w3, w2, bsz, seq, cap):
    n = x_mid.shape[0]
    return pl.pallas_call(
        functools.partial(_experts_kernel, cap=cap, seq=seq),
        grid=(N_EXPERTS, bsz),
        in_specs=[pl.BlockSpec((None, 1, cap), lambda e, b: (b * N_EXPERTS + e, 0, 0), memory_space=pltpu.SMEM),
                  pl.BlockSpec(memory_space=pl.ANY),
                  pl.BlockSpec(memory_space=pl.ANY),
                  pl.BlockSpec((None, D_MODEL, D_FF), lambda e, b: (e, 0, 0)),
                  pl.BlockSpec((None, D_MODEL, D_FF), lambda e, b: (e, 0, 0)),
                  pl.BlockSpec((None, D_FF, D_MODEL), lambda e, b: (e, 0, 0))],
        out_specs=pl.BlockSpec(memory_space=pl.ANY),
        out_shape=jax.ShapeDtypeStruct((n, D_MODEL), F32),
        scratch_shapes=[pltpu.VMEM((cap, Z_W), F32), pltpu.VMEM((cap, D_MODEL), F32),
                        pltpu.SemaphoreType.DMA((3,))],
        input_output_aliases={2: 0},
        compiler_params=_cparams(2),
    )(idx.reshape(bsz * N_EXPERTS, 1, cap), z, x_mid, w1, w3, w2)


def _repack_w_in(w_in):
    mlstm_w = 2 * M_HEADS * M_DQK + 2 * M_HEADS * M_DV
    pad = jnp.zeros(w_in.shape[:2] + (LANES - N_GATES,), w_in.dtype)
    return jnp.concatenate([w_in[..., :mlstm_w + N_GATES], pad, w_in[..., mlstm_w + N_GATES:]], axis=-1).astype(BF16)


def kernel(x, mem, g_mix, w_in, b_gates, conv_qk, g_mlstm_head, na_gq, na_gk, na_rpb, g_mem, w_mem_kv, mem_gq,
           mem_gk, w_out, g_ffn, w_router, w1, w3, w2):
    bsz, seq, _ = x.shape
    depth = w_in.shape[0]
    rows = seq // GRID_W
    cap = EC_CAPACITY * seq // N_EXPERTS
    tm = 512

    w_in_b = _repack_w_in(w_in)
    w_out_b = w_out.astype(BF16)
    w_kv_b = w_mem_kv.astype(BF16)
    w1_b, w3_b, w2_b = w1.astype(BF16), w3.astype(BF16), w2.astype(BF16)
    b_gate_p = jnp.pad(b_gates, ((0, 0), (0, LANES - N_GATES)))
    w_router_p = jnp.pad(jnp.concatenate(_split_bf16(w_router, 3), axis=-1),
                         ((0, 0), (0, 0), (0, LANES - 3 * N_EXPERTS)))
    na_bias = jax.vmap(lambda r: _na_bias_table(r, rows))(na_rpb)

    x2 = x.reshape(bsz * seq, D_MODEL)
    for l in range(depth):
        qk, v, o, gt, na, cq = _inproj(x2, g_mix[l][None], w_in_b[l], tm)
        h_fw, h_bw = _mlstm(qk, v, gt, conv_qk[l], b_gate_p[l][None], bsz, seq)
        y_n = _neighbourhood_attention(na, na_bias[l], jnp.tile(na_gq[l], NA_HEADS)[None],
                                       jnp.tile(na_gk[l], NA_HEADS)[None], bsz, seq)
        k_m, v_m = _mem_kv(mem, g_mem[l][None], w_kv_b[l], jnp.tile(mem_gk[l], MEM_HEADS)[None])
        y_c = _mem_attention(cq, k_m, v_m, jnp.tile(mem_gq[l], MEM_HEADS)[None], bsz, seq, tm)
        x_mid, z, aff_t = _outproj(h_fw, h_bw, o, y_n, y_c, x2, w_out_b[l], g_mlstm_head[l][None], g_ffn[l][None],
                                   w_router_p[l], tm)
        idx = _select(aff_t, bsz, seq, cap)
        x2 = _experts(idx, z, x_mid, w1_b[l], w3_b[l], w2_b[l], bsz, seq, cap)
    return x2.reshape(bsz, seq, D_MODEL)
```

```python
import functools

import numpy as np
import jax
import jax.numpy as jnp
from jax import lax
from jax.experimental import pallas as pl
from jax.experimental.pallas import tpu as pltpu

F32 = jnp.float32
BF16 = jnp.bfloat16

D_MODEL = 1024
GRID_W = 64
M_HEADS = 4
M_DQK = 64
M_DV = 128
M_CHUNK = 128
M_CONV = 5
NA_HEADS = 4
NA_DH = 64
NA_KH = 8
NA_KW = 16
MEM_HEADS = 4
MEM_DH = 64
N_EXPERTS = 16
EC_CAPACITY = 2
D_FF = 2048
N_GATES = 4 * M_HEADS
EPS = 1e-6

LANES = 128
QK_W = 2 * M_HEADS * M_DQK
V_W = M_HEADS * M_DV
NA_W = NA_HEADS * NA_DH
MEM_W = MEM_HEADS * MEM_DH
SEG_QK = (0, QK_W)
SEG_V = (SEG_QK[1], SEG_QK[1] + V_W)
SEG_O = (SEG_V[1], SEG_V[1] + V_W)
SEG_G = (SEG_O[1], SEG_O[1] + LANES)
SEG_NA = (SEG_G[1], SEG_G[1] + 3 * NA_W)
SEG_CQ = (SEG_NA[1], SEG_NA[1] + MEM_W)
PROJ_W = SEG_CQ[1]
XZ_X = (0, D_MODEL)
XZ_H = (XZ_X[1], XZ_X[1] + D_MODEL)
XZ_A = (XZ_H[1], XZ_H[1] + LANES)
XZ_W = XZ_A[1]
NA_QROWS = 4
NA_KROWS = NA_QROWS + NA_KH
NEG_BIG = -1e30
VMEM_LIMIT = 56 * 1024 * 1024


def _cparams(n_axes):
    return pltpu.CompilerParams(dimension_semantics=("arbitrary",) * n_axes, vmem_limit_bytes=VMEM_LIMIT)


def _dot(a, b):
    return jnp.dot(a, b, preferred_element_type=F32)


def _dot_nt(a, b):
    return lax.dot_general(a, b, (((1,), (1,)), ((), ())), preferred_element_type=F32)


def _dot_exact_rhs(a, b_bf16):
    a0, a1, a2 = _split_bf16(a, 3)
    return _dot(a0, b_bf16) + _dot(a1, b_bf16) + _dot(a2, b_bf16)


def _split_bf16(w, terms):
    parts, r = [], w
    for _ in range(terms):
        p = r.astype(BF16)
        parts.append(p)
        r = r - p.astype(F32)
    return parts


def _head_mask(width, head_dim, h):
    lane = lax.broadcasted_iota(jnp.int32, (1, width), 1)
    return (lane // head_dim) == h


def _same_head_matrix(width, head_dim):
    r = lax.broadcasted_iota(jnp.int32, (width, width), 0) // head_dim
    c = lax.broadcasted_iota(jnp.int32, (width, width), 1) // head_dim
    return jnp.where(r == c, 1.0, 0.0).astype(BF16)


def _per_head_rms(x, gain, head_dim):
    ss = _dot_exact_rhs(x * x, _same_head_matrix(x.shape[1], head_dim))
    return x * lax.rsqrt(ss * (1.0 / head_dim) + EPS) * gain


def _rms_rows(x, gain):
    ms = jnp.mean(x * x, axis=-1, keepdims=True)
    return x * lax.rsqrt(ms + EPS) * gain


def _softmax_rows(s):
    m = jnp.max(s, axis=-1, keepdims=True)
    e = jnp.exp(s - m)
    return e * (1.0 / jnp.sum(e, axis=-1, keepdims=True))


def _inproj_kernel(x_ref, g_ref, w_ref, qk_ref, v_ref, o_ref, gt_ref, na_ref, cq_ref):
    h = _rms_rows(x_ref[...], g_ref[...]).astype(BF16)
    for seg, out in ((SEG_QK, qk_ref), (SEG_V, v_ref), (SEG_O, o_ref), (SEG_G, gt_ref), (SEG_NA, na_ref),
                     (SEG_CQ, cq_ref)):
        out[...] = _dot(h, w_ref[:, seg[0]:seg[1]])


def _inproj(x_src, g, w, layer, tm):
    n = x_src.shape[0]
    widths = [s[1] - s[0] for s in (SEG_QK, SEG_V, SEG_O, SEG_G, SEG_NA, SEG_CQ)]
    return pl.pallas_call(
        _inproj_kernel,
        grid=(n // tm,),
        in_specs=[pl.BlockSpec((tm, D_MODEL), lambda i: (i, 0)),
                  pl.BlockSpec((None, 1, D_MODEL), lambda i: (layer, 0, 0)),
                  pl.BlockSpec((None, D_MODEL, PROJ_W), lambda i: (layer, 0, 0))],
        out_specs=[pl.BlockSpec((tm, w_), lambda i: (i, 0)) for w_ in widths],
        out_shape=[jax.ShapeDtypeStruct((n, w_), F32) for w_ in widths],
        compiler_params=_cparams(1),
    )(x_src, g, w)


HALO = 8


def _log_sigmoid(x):
    return -(jnp.maximum(-x, 0.0) + jnp.log(1.0 + jnp.exp(-jnp.abs(x))))


def _mlstm_direction(reverse, chunk, n_chunks, qk_ref, prev_ref, next_ref, v_ref, g_ref, cw_ref, bg_ref, h_ref,
                     c_ref, n_ref, m_ref):
    L = M_CHUNK
    prev_on = jnp.where(chunk > 0, 1.0, 0.0)
    next_on = jnp.where(chunk < n_chunks - 1, 1.0, 0.0)
    ext = jnp.concatenate([prev_ref[...] * prev_on, qk_ref[...], next_ref[...] * next_on], axis=0)
    pad = M_CONV // 2
    conv = ext[HALO - pad:HALO - pad + L] * cw_ref[0:1, :]
    for j in range(1, M_CONV):
        conv = conv + ext[HALO - pad + j:HALO - pad + j + L] * cw_ref[j:j + 1, :]
    qk = conv * jax.nn.sigmoid(conv)
    q = qk[:, :QK_W // 2] * (M_DQK ** -0.5)
    k = qk[:, QK_W // 2:]
    k_b = k.astype(BF16)

    gates = g_ref[...] + bg_ref[...]
    log_f = _log_sigmoid(gates)
    t_i = lax.broadcasted_iota(jnp.int32, (L, L), 0)
    s_i = lax.broadcasted_iota(jnp.int32, (L, L), 1)
    visible = (s_i >= t_i) if reverse else (s_i <= t_i)
    b_all = _masked_prefix_sum(visible, log_f)
    b_all_t = b_all.T
    gates_t = gates.T
    off = 2 * M_HEADS if reverse else 0
    last = 0 if reverse else L - 1

    for h in range(M_HEADS):
        hm = _head_mask(QK_W // 2, M_DQK, h)
        gi = off + h
        gf = off + M_HEADS + h
        b_col = b_all[:, gf:gf + 1]
        b_row = b_all_t[gf:gf + 1, :]
        li_col = gates[:, gi:gi + 1]
        li_row = gates_t[gi:gi + 1, :]
        g_tot = b_all[last:last + 1, gf:gf + 1]
        sidx = (4 if reverse else 0) + h
        m_prev = m_ref[sidx:sidx + 1, 0:1]
        c_prev = c_ref[sidx]
        n_prev = n_ref[sidx:sidx + 1, :]

        q_h = jnp.where(hm, q, 0.0)
        q_hb = q_h.astype(BF16)
        v_hb = v_ref[:, h * M_DV:(h + 1) * M_DV].astype(BF16)
        d_log = jnp.where(visible, b_col - b_row + li_row, -jnp.inf)
        inter = b_col + m_prev
        m_t = jnp.maximum(inter, jnp.max(d_log, axis=-1, keepdims=True))
        s_inter = jnp.exp(inter - m_t)
        scores = _dot_nt(q_hb, k_b) * jnp.exp(d_log - m_t)
        num = s_inter * _dot(q_hb, c_prev.astype(BF16)) + _dot(scores.astype(BF16), v_hb)
        den = s_inter * jnp.sum(q_h * n_prev, axis=-1, keepdims=True) + jnp.sum(scores, axis=-1, keepdims=True)
        h_ref[:, h * M_DV:(h + 1) * M_DV] = num / jnp.maximum(jnp.abs(den), jnp.exp(-m_t))

        w_log = g_tot - b_col + li_col
        a = jnp.max(w_log, axis=0, keepdims=True)
        kw = jnp.where(hm, k, 0.0) * jnp.exp(w_log - a)
        kv = _dot(kw.T.astype(BF16), v_hb)
        n_c = jnp.sum(kw, axis=0, keepdims=True)
        m_new = jnp.maximum(g_tot + m_prev, a)
        s_prev = jnp.exp(g_tot + m_prev - m_new)
        s_cur = jnp.exp(a - m_new)
        c_ref[sidx] = s_prev * c_prev + s_cur * kv
        n_ref[sidx:sidx + 1, :] = s_prev * n_prev + s_cur * n_c
        m_ref[sidx:sidx + 1, :] = jnp.broadcast_to(m_new, (1, LANES))


def _masked_prefix_sum(visible, x):
    tri = jnp.where(visible, 1.0, 0.0).astype(BF16)
    x0, x1, x2 = _split_bf16(x, 3)
    return _dot(tri, x0) + _dot(tri, x1) + _dot(tri, x2)


def _mlstm_kernel(qkf, pvf, nxf, vf, gf, qkb, pvb, nxb, vb, gb, cw_ref, bg_ref, hf_ref, hb_ref, c_ref, n_ref, m_ref):
    c = pl.program_id(1)
    n_chunks = pl.num_programs(1)

    @pl.when(c == 0)
    def _():
        c_ref[...] = jnp.zeros_like(c_ref)
        n_ref[...] = jnp.zeros_like(n_ref)
        m_ref[...] = jnp.zeros_like(m_ref)

    _mlstm_direction(False, c, n_chunks, qkf, pvf, nxf, vf, gf, cw_ref, bg_ref, hf_ref, c_ref, n_ref, m_ref)
    _mlstm_direction(True, n_chunks - 1 - c, n_chunks, qkb, pvb, nxb, vb, gb, cw_ref, bg_ref, hb_ref, c_ref, n_ref,
                     m_ref)


def _mlstm(qk, v, gt, conv_w, b_gate, layer, bsz, seq):
    L = M_CHUNK
    nc = seq // L
    hb = L // HALO
    n_halo = seq // HALO
    qk3 = qk.reshape(bsz, seq, QK_W)
    v3 = v.reshape(bsz, seq, V_W)
    g3 = gt.reshape(bsz, seq, LANES)

    def specs(cidx):
        return [
            pl.BlockSpec((None, L, QK_W), lambda b, c: (b, cidx(c), 0)),
            pl.BlockSpec((None, HALO, QK_W), lambda b, c: (b, jnp.maximum(cidx(c) * hb - 1, 0), 0)),
            pl.BlockSpec((None, HALO, QK_W), lambda b, c: (b, jnp.minimum((cidx(c) + 1) * hb, n_halo - 1), 0)),
            pl.BlockSpec((None, L, V_W), lambda b, c: (b, cidx(c), 0)),
            pl.BlockSpec((None, L, LANES), lambda b, c: (b, cidx(c), 0)),
        ]

    fw = lambda c: c
    bw = lambda c: nc - 1 - c
    h_fw, h_bw = pl.pallas_call(
        _mlstm_kernel,
        grid=(bsz, nc),
        in_specs=specs(fw) + specs(bw) + [pl.BlockSpec((None, M_CONV, QK_W), lambda b, c: (layer, 0, 0)),
                                          pl.BlockSpec((None, 1, LANES), lambda b, c: (layer, 0, 0))],
        out_specs=[pl.BlockSpec((None, L, V_W), lambda b, c: (b, c, 0)),
                   pl.BlockSpec((None, L, V_W), lambda b, c: (b, nc - 1 - c, 0))],
        out_shape=[jax.ShapeDtypeStruct((bsz, seq, V_W), F32)] * 2,
        scratch_shapes=[pltpu.VMEM((2 * M_HEADS, QK_W // 2, M_DV), F32),
                        pltpu.VMEM((2 * M_HEADS, QK_W // 2), F32),
                        pltpu.VMEM((2 * M_HEADS, LANES), F32)],
        compiler_params=_cparams(2),
    )(qk3, qk3, qk3, v3, g3, qk3, qk3, qk3, v3, g3, conv_w, b_gate)
    return h_fw.reshape(bsz * seq, V_W), h_bw.reshape(bsz * seq, V_W)


def _na_kernel(q_ref, k0, k1, k2, v0, v1, v2, bias_ref, gq_ref, gk_ref, o_ref):
    q = _per_head_rms(q_ref[...], gq_ref[...], NA_DH) * (NA_DH ** -0.5)
    k = jnp.concatenate([k0[...], k1[...], k2[...]], axis=0)
    k_b = _per_head_rms(k, gk_ref[...], NA_DH).astype(BF16)
    v = jnp.concatenate([v0[...], v1[...], v2[...]], axis=0)
    acc = jnp.zeros(o_ref.shape, F32)
    for h in range(NA_HEADS):
        hm = _head_mask(NA_W, NA_DH, h)
        s = _dot_nt(jnp.where(hm, q, 0.0).astype(BF16), k_b) + bias_ref[h]
        p = _softmax_rows(s).astype(BF16)
        acc = acc + _dot(p, jnp.where(hm, v, 0.0).astype(BF16))
    o_ref[...] = acc


def _na_bias_selectors(rows):
    nj = rows // NA_QROWS
    qi = np.arange(NA_QROWS)
    ki = np.arange(NA_KROWS)
    row_sel = np.zeros((3, NA_QROWS, NA_KROWS, 2 * NA_KH - 1), np.float32)
    row_ok = np.zeros((3, NA_QROWS, NA_KROWS), bool)
    for pat, j in enumerate((0, 1, nj - 1)):
        ks = min(max(NA_QROWS * j - NA_KH // 2, 0), rows - NA_KROWS)
        r = NA_QROWS * j + qi
        rs = np.clip(r - NA_KH // 2, 0, rows - NA_KH)
        krow = ks + ki
        ok = (krow[None, :] >= rs[:, None]) & (krow[None, :] < rs[:, None] + NA_KH)
        dr = krow[None, :] - r[:, None] + NA_KH - 1
        for a in range(NA_QROWS):
            for c in range(NA_KROWS):
                if ok[a, c]:
                    row_sel[pat, a, c, dr[a, c]] = 1.0
        row_ok[pat] = ok
    qc = np.arange(GRID_W)
    cs = np.clip(qc - NA_KW // 2, 0, GRID_W - NA_KW)
    col_ok = (qc[None, :] >= cs[:, None]) & (qc[None, :] < cs[:, None] + NA_KW)
    dc = qc[None, :] - qc[:, None] + NA_KW - 1
    col_sel = np.zeros((2 * NA_KW - 1, GRID_W, GRID_W), np.float32)
    for a in range(GRID_W):
        for c in range(GRID_W):
            if col_ok[a, c]:
                col_sel[dc[a, c], a, c] = 1.0
    ok = row_ok[:, :, None, :, None] & col_ok[None, None, :, None, :]
    return (row_sel.reshape(3 * NA_QROWS * NA_KROWS, -1), col_sel.reshape(2 * NA_KW - 1, -1),
            ok.reshape(3, NA_QROWS * GRID_W, NA_KROWS * GRID_W))


def _na_bias_table(rpb, rows):
    row_sel, col_sel, ok = _na_bias_selectors(rows)
    hi = lax.Precision.HIGHEST
    picked_rows = jnp.einsum('gr,hrd->hgd', row_sel, rpb, precision=hi)
    band = jnp.einsum('hgd,dn->hgn', picked_rows, col_sel, precision=hi)
    band = band.reshape(-1, 3, NA_QROWS, NA_KROWS, GRID_W, GRID_W).transpose(0, 1, 2, 4, 3, 5)
    return jnp.where(ok[None], band.reshape(-1, 3, NA_QROWS * GRID_W, NA_KROWS * GRID_W), NEG_BIG)


def _neighbourhood_attention(na, bias, gq, gk, layer, bsz, seq):
    rows = seq // GRID_W
    nj = rows // NA_QROWS
    qn = NA_QROWS * GRID_W
    kn = NA_KROWS * GRID_W
    na3 = na.reshape(bsz, seq, 3 * NA_W)
    kstart = lambda j: jnp.clip(j - 1, 0, nj - 3)
    pattern = lambda j: jnp.minimum(j, 1) + jnp.maximum(j - (nj - 2), 0)
    kv_specs = [pl.BlockSpec((None, qn, NA_W), functools.partial(lambda j, b, i, part: (b, kstart(j) + i, part),
                                                                 i=i, part=part))
                for part in (1, 2) for i in range(3)]
    out = pl.pallas_call(
        _na_kernel,
        grid=(nj, bsz),
        in_specs=[pl.BlockSpec((None, qn, NA_W), lambda j, b: (b, j, 0))] + kv_specs + [
            pl.BlockSpec((None, NA_HEADS, None, qn, kn), lambda j, b: (layer, 0, pattern(j), 0, 0)),
            pl.BlockSpec((None, 1, NA_W), lambda j, b: (layer, 0, 0)),
            pl.BlockSpec((None, 1, NA_W), lambda j, b: (layer, 0, 0))],
        out_specs=pl.BlockSpec((None, qn, NA_W), lambda j, b: (b, j, 0)),
        out_shape=jax.ShapeDtypeStruct((bsz, seq, NA_W), F32),
        compiler_params=_cparams(2),
    )(na3, na3, na3, na3, na3, na3, na3, bias, gq, gk)
    return out.reshape(bsz * seq, NA_W)


def _mem_kv_kernel(mem_ref, g_ref, w_ref, gk_ref, k_ref, v_ref):
    h = _rms_rows(mem_ref[...], g_ref[...]).astype(BF16)
    kv = _dot(h, w_ref[...])
    k_ref[...] = _per_head_rms(kv[:, :MEM_W], gk_ref[...], MEM_DH).astype(BF16)
    v_ref[...] = kv[:, MEM_W:].astype(BF16)


def _mem_kv(mem, g_mem, w_kv, gk, layer):
    bsz, n_mem, _ = mem.shape
    return pl.pallas_call(
        _mem_kv_kernel,
        grid=(bsz,),
        in_specs=[pl.BlockSpec((None, n_mem, D_MODEL), lambda b: (b, 0, 0)),
                  pl.BlockSpec((None, 1, D_MODEL), lambda b: (layer, 0, 0)),
                  pl.BlockSpec((None, D_MODEL, 2 * MEM_W), lambda b: (layer, 0, 0)),
                  pl.BlockSpec((None, 1, MEM_W), lambda b: (layer, 0, 0))],
        out_specs=[pl.BlockSpec((None, n_mem, MEM_W), lambda b: (b, 0, 0))] * 2,
        out_shape=[jax.ShapeDtypeStruct((bsz, n_mem, MEM_W), BF16)] * 2,
        compiler_params=_cparams(1),
    )(mem, g_mem, w_kv, gk)


def _mem_attn_kernel(q_ref, k_ref, v_ref, gq_ref, o_ref):
    q = _per_head_rms(q_ref[...], gq_ref[...], MEM_DH) * (MEM_DH ** -0.5)
    k_b = k_ref[...]
    v_b = v_ref[...]
    acc = jnp.zeros(o_ref.shape, F32)
    for h in range(MEM_HEADS):
        hm = _head_mask(MEM_W, MEM_DH, h)
        p = _softmax_rows(_dot_nt(jnp.where(hm, q, 0.0).astype(BF16), k_b)).astype(BF16)
        acc = acc + _dot(p, jnp.where(hm, v_b, jnp.zeros_like(v_b)))
    o_ref[...] = acc


def _mem_attention(cq, k_m, v_m, gq, layer, bsz, seq, tq):
    n_mem = k_m.shape[1]
    out = pl.pallas_call(
        _mem_attn_kernel,
        grid=(bsz, seq // tq),
        in_specs=[pl.BlockSpec((None, tq, MEM_W), lambda b, i: (b, i, 0)),
                  pl.BlockSpec((None, n_mem, MEM_W), lambda b, i: (b, 0, 0)),
                  pl.BlockSpec((None, n_mem, MEM_W), lambda b, i: (b, 0, 0)),
                  pl.BlockSpec((None, 1, MEM_W), lambda b, i: (layer, 0, 0))],
        out_specs=pl.BlockSpec((None, tq, MEM_W), lambda b, i: (b, i, 0)),
        out_shape=jax.ShapeDtypeStruct((bsz, seq, MEM_W), F32),
        compiler_params=_cparams(2),
    )(cq.reshape(bsz, seq, MEM_W), k_m, v_m, gq)
    return out.reshape(bsz * seq, MEM_W)


def _outproj_kernel(hf_ref, hb_ref, o_ref, yn_ref, yc_ref, x_ref, w_ref, gh_ref, gf_ref, wr_ref, xz_ref, afft_ref):
    hs = hf_ref[...] + hb_ref[...]
    acc = x_ref[...]
    for h in range(M_HEADS):
        sl = slice(h * M_DV, (h + 1) * M_DV)
        y = jax.nn.sigmoid(o_ref[:, sl]) * _rms_rows(hs[:, sl], gh_ref[:, sl])
        acc = acc + _dot(y.astype(BF16), w_ref[sl, :])
    acc = acc + _dot(yn_ref[...].astype(BF16), w_ref[V_W:V_W + NA_W, :])
    acc = acc + _dot(yc_ref[...].astype(BF16), w_ref[V_W + NA_W:, :])
    xz_ref[:, XZ_X[0]:XZ_X[1]] = acc
    h2 = _rms_rows(acc, gf_ref[...])
    xz_ref[:, XZ_H[0]:XZ_H[1]] = h2
    h2_hi, h2_lo = _split_bf16(h2, 2)
    part = _dot(h2_hi, wr_ref[...]) + _dot(h2_lo, wr_ref[...])
    logits = part + pltpu.roll(part, LANES - N_EXPERTS, 1) + pltpu.roll(part, LANES - 2 * N_EXPERTS, 1)
    lane = lax.broadcasted_iota(jnp.int32, logits.shape, 1)
    aff = _softmax_rows(jnp.where(lane < N_EXPERTS, logits, NEG_BIG))
    xz_ref[:, XZ_A[0]:XZ_A[1]] = aff
    afft_ref[...] = aff.T[:N_EXPERTS, :]


def _outproj(h_fw, h_bw, o, y_n, y_c, x_src, w_out, g_head, g_ffn, w_router_p, layer, tm):
    n = x_src.shape[0]
    row = lambda w_: pl.BlockSpec((tm, w_), lambda i: (i, 0))
    per_layer = lambda a: pl.BlockSpec((None,) + a.shape[1:], lambda i: (layer, 0, 0))
    return pl.pallas_call(
        _outproj_kernel,
        grid=(n // tm,),
        in_specs=[row(V_W), row(V_W), row(V_W), row(NA_W), row(MEM_W), row(D_MODEL), per_layer(w_out),
                  per_layer(g_head), per_layer(g_ffn), per_layer(w_router_p)],
        out_specs=[row(XZ_W), pl.BlockSpec((N_EXPERTS, tm), lambda i: (0, i))],
        out_shape=[jax.ShapeDtypeStruct((n, XZ_W), F32), jax.ShapeDtypeStruct((N_EXPERTS, n), F32)],
        compiler_params=_cparams(1),
    )(h_fw, h_bw, o, y_n, y_c, x_src, w_out, g_head, g_ffn, w_router_p)


def _lane_cumsum(mask_f, tri_b):
    n = mask_f.shape[1]
    run = jnp.zeros((mask_f.shape[0], 1), F32)
    parts = []
    for j in range(n // LANES):
        local = _dot(mask_f[:, j * LANES:(j + 1) * LANES].astype(BF16), tri_b) + run
        parts.append(local)
        run = local[:, LANES - 1:LANES]
    return jnp.concatenate(parts, axis=1)


def _select_kernel(aff_ref, idx_ref, csum_ref, *, cap):
    bits = pltpu.bitcast(aff_ref[...], jnp.int32)
    n_e, n_tok = bits.shape
    thr = jnp.zeros((n_e, 1), jnp.int32)
    for bit in range(30, -1, -1):
        cand = thr | (1 << bit)
        cnt = jnp.sum(jnp.where(bits >= cand, 1.0, 0.0), axis=1, keepdims=True)
        thr = jnp.where(cnt >= cap, cand, thr)
    s_i = lax.broadcasted_iota(jnp.int32, (LANES, LANES), 0)
    t_i = lax.broadcasted_iota(jnp.int32, (LANES, LANES), 1)
    tri_b = jnp.where(s_i <= t_i, 1.0, 0.0).astype(BF16)
    gt = bits > thr
    eq_f = jnp.where(bits == thr, 1.0, 0.0)
    need = cap - jnp.sum(jnp.where(gt, 1.0, 0.0), axis=1, keepdims=True)
    eq_rank = _lane_cumsum(eq_f, tri_b)
    sel_f = jnp.where(gt | ((eq_f > 0.5) & (eq_rank <= need)), 1.0, 0.0)
    csum = _lane_cumsum(sel_f, tri_b)
    for e in range(n_e):
        csum_ref[e] = csum[e:e + 1, :]
    slot = lax.broadcasted_iota(jnp.int32, (cap, 1), 0).astype(F32)
    lane = lax.broadcasted_iota(jnp.int32, (cap, LANES), 1)

    def per_expert(e, cols):
        acc = jnp.zeros((cap, LANES), F32)
        for j in range(n_tok // LANES):
            acc = acc + jnp.where(csum_ref[e, :, j * LANES:(j + 1) * LANES] <= slot, 1.0, 0.0)
        return jnp.where(lane == e, jnp.sum(acc, axis=1, keepdims=True), cols)

    cols = lax.fori_loop(0, n_e, per_expert, jnp.zeros((cap, LANES), F32))
    idx_ref[...] = cols.T[:n_e, :].astype(jnp.int32)


def _select(aff_t, bsz, seq, cap):
    return pl.pallas_call(
        functools.partial(_select_kernel, cap=cap),
        grid=(bsz,),
        in_specs=[pl.BlockSpec((N_EXPERTS, seq), lambda b: (0, b))],
        out_specs=pl.BlockSpec((None, N_EXPERTS, cap), lambda b: (b, 0, 0)),
        out_shape=jax.ShapeDtypeStruct((bsz, N_EXPERTS, cap), jnp.int32),
        scratch_shapes=[pltpu.VMEM((N_EXPERTS, 1, seq), F32)],
        compiler_params=_cparams(1),
    )(aff_t)


FF_CHUNK = 512


N_ROWBUF = 3


def _experts_kernel(idx_prev, idx_cur, idx_next, xz_hbm, w1_ref, w3_ref, w2_ref, out_hbm, buf, gsem, ssem, *, cap,
                    seq):
    del xz_hbm
    e, b = pl.program_id(0), pl.program_id(1)
    n_e, n_b = pl.num_programs(0), pl.num_programs(1)
    k = e * n_b + b
    slot_cur = lax.rem(k, N_ROWBUF)
    slot_next = lax.rem(k + 1, N_ROWBUF)
    slot_prev = lax.rem(k + 2, N_ROWBUF)
    base_cur = b * seq
    base_next = jnp.where(b + 1 == n_b, 0, b + 1) * seq
    base_prev = jnp.where(b == 0, n_b - 1, b - 1) * seq

    def fetch_row(idx_ref, base, slot, i):
        return pltpu.make_async_copy(out_hbm.at[pl.ds(base + idx_ref[0, i], 1)], buf.at[slot, pl.ds(i, 1)],
                                     gsem.at[slot])

    def writeback_row(idx_ref, base, slot, i):
        return pltpu.make_async_copy(buf.at[slot, pl.ds(i, 1), pl.ds(XZ_X[0], D_MODEL)],
                                     out_hbm.at[pl.ds(base + idx_ref[0, i], 1), pl.ds(XZ_X[0], D_MODEL)],
                                     ssem.at[slot])

    def wait_fetched(slot):
        pltpu.make_async_copy(out_hbm.at[pl.ds(0, cap)], buf.at[slot], gsem.at[slot]).wait()

    def wait_written(slot):
        pltpu.make_async_copy(buf.at[slot, :, pl.ds(XZ_X[0], D_MODEL)],
                              out_hbm.at[pl.ds(0, cap), pl.ds(XZ_X[0], D_MODEL)], ssem.at[slot]).wait()

    def start_all(make_copy):
        def body(i, carry):
            make_copy(i).start()
            return carry
        lax.fori_loop(0, cap, body, 0)

    @pl.when(k == 0)
    def _():
        start_all(lambda i: fetch_row(idx_prev, base_prev, slot_prev, i))
        wait_fetched(slot_prev)
        start_all(lambda i: fetch_row(idx_cur, base_cur, slot_cur, i))

    wait_fetched(slot_cur)
    xb = buf[slot_cur, :, XZ_H[0]:XZ_H[1]].astype(BF16)
    lane = lax.broadcasted_iota(jnp.int32, (cap, LANES), 1)
    gate = jnp.sum(jnp.where(lane == e, buf[slot_cur, :, XZ_A[0]:XZ_A[1]], 0.0), axis=1, keepdims=True)
    ye = jnp.zeros((cap, D_MODEL), F32)
    n_ff = D_FF // FF_CHUNK
    per_chunk = 2 * cap // n_ff
    for f in range(n_ff):
        for i in range(f * per_chunk, (f + 1) * per_chunk):
            if i < cap:
                fetch_row(idx_next, base_next, slot_next, i).start()
            else:
                writeback_row(idx_prev, base_prev, slot_prev, i - cap).start()
        sl = slice(f * FF_CHUNK, (f + 1) * FF_CHUNK)
        h1 = _dot(xb, w1_ref[:, sl])
        h3 = _dot(xb, w3_ref[:, sl])
        hid = (h1 * jax.nn.sigmoid(h1) * h3).astype(BF16)
        ye = ye + _dot(hid, w2_ref[sl, :])
    buf[slot_cur, :, XZ_X[0]:XZ_X[1]] = buf[slot_cur, :, XZ_X[0]:XZ_X[1]] + ye * gate
    wait_written(slot_prev)

    @pl.when(k == n_e * n_b - 1)
    def _():
        wait_fetched(slot_next)
        start_all(lambda i: writeback_row(idx_cur, base_cur, slot_cur, i))
        wait_written(slot_cur)


def _experts(idx, xz, w1, w3, w2, layer, bsz, seq, cap):
    assert bsz >= N_ROWBUF, "rows in flight for neighbouring steps must belong to different sequences"
    idx3 = idx.reshape(bsz * N_EXPERTS, 1, cap)

    def idx_spec(step):
        def index_map(e, b):
            b2 = lax.rem(b + step + bsz, bsz)
            e2 = jnp.clip(e + (b + step + bsz) // bsz - 1, 0, N_EXPERTS - 1)
            return (b2 * N_EXPERTS + e2, 0, 0)
        return pl.BlockSpec((None, 1, cap), index_map, memory_space=pltpu.SMEM)

    w_up = pl.BlockSpec((None, None, D_MODEL, D_FF), lambda e, b: (layer, e, 0, 0))
    w_down = pl.BlockSpec((None, None, D_FF, D_MODEL), lambda e, b: (layer, e, 0, 0))
    return pl.pallas_call(
        functools.partial(_experts_kernel, cap=cap, seq=seq),
        grid=(N_EXPERTS, bsz),
        in_specs=[idx_spec(-1), idx_spec(0), idx_spec(1), pl.BlockSpec(memory_space=pl.ANY), w_up, w_up, w_down],
        out_specs=pl.BlockSpec(memory_space=pl.ANY),
        out_shape=jax.ShapeDtypeStruct(xz.shape, F32),
        scratch_shapes=[pltpu.VMEM((N_ROWBUF, cap, XZ_W), F32), pltpu.SemaphoreType.DMA((N_ROWBUF,)),
                        pltpu.SemaphoreType.DMA((N_ROWBUF,))],
        input_output_aliases={3: 0},
        compiler_params=_cparams(2),
    )(idx3, idx3, idx3, xz, w1, w3, w2)


def _repack_w_in(w_in):
    mlstm_w = 2 * M_HEADS * M_DQK + 2 * M_HEADS * M_DV
    pad = jnp.zeros(w_in.shape[:2] + (LANES - N_GATES,), w_in.dtype)
    return jnp.concatenate([w_in[..., :mlstm_w + N_GATES], pad, w_in[..., mlstm_w + N_GATES:]], axis=-1).astype(BF16)


def kernel(x, mem, g_mix, w_in, b_gates, conv_qk, g_mlstm_head, na_gq, na_gk, na_rpb, g_mem, w_mem_kv, mem_gq,
           mem_gk, w_out, g_ffn, w_router, w1, w3, w2):
    bsz, seq, _ = x.shape
    depth = w_in.shape[0]
    rows = seq // GRID_W
    cap = EC_CAPACITY * seq // N_EXPERTS
    tm = 512

    w_in_b = _repack_w_in(w_in)
    w_out_b = w_out.astype(BF16)
    w_kv_b = w_mem_kv.astype(BF16)
    w1_b, w3_b, w2_b = w1.astype(BF16), w3.astype(BF16), w2.astype(BF16)
    b_gate_p = jnp.pad(b_gates, ((0, 0), (0, LANES - N_GATES)))[:, None, :]
    w_router_p = jnp.pad(jnp.concatenate(_split_bf16(w_router, 3), axis=-1),
                         ((0, 0), (0, 0), (0, LANES - 3 * N_EXPERTS)))
    na_bias = jax.vmap(lambda r: _na_bias_table(r, rows))(na_rpb)
    row3 = lambda a: a[:, None, :]
    per_head = lambda a, n_heads: row3(jnp.tile(a, (1, n_heads)))
    g_mix3, g_mem3, g_head3, g_ffn3 = row3(g_mix), row3(g_mem), row3(g_mlstm_head), row3(g_ffn)
    na_gq3, na_gk3 = per_head(na_gq, NA_HEADS), per_head(na_gk, NA_HEADS)
    mem_gq3, mem_gk3 = per_head(mem_gq, MEM_HEADS), per_head(mem_gk, MEM_HEADS)

    xs = x.reshape(bsz * seq, D_MODEL)
    for l in range(depth):
        qk, v, o, gt, na, cq = _inproj(xs, g_mix3, w_in_b, l, tm)
        h_fw, h_bw = _mlstm(qk, v, gt, conv_qk, b_gate_p, l, bsz, seq)
        y_n = _neighbourhood_attention(na, na_bias, na_gq3, na_gk3, l, bsz, seq)
        k_m, v_m = _mem_kv(mem, g_mem3, w_kv_b, mem_gk3, l)
        y_c = _mem_attention(cq, k_m, v_m, mem_gq3, l, bsz, seq, tm)
        xz, aff_t = _outproj(h_fw, h_bw, o, y_n, y_c, xs, w_out_b, g_head3, g_ffn3, w_router_p, l, tm)
        idx = _select(aff_t, bsz, seq, cap)
        xs = _experts(idx, xz, w1_b, w3_b, w2_b, l, bsz, seq, cap)
    return xs[:, XZ_X[0]:XZ_X[1]].reshape(bsz, seq, D_MODEL)
```

```python
import functools

import numpy as np
import jax
import jax.numpy as jnp
from jax import lax
from jax.experimental import pallas as pl
from jax.experimental.pallas import tpu as pltpu

F32 = jnp.float32
BF16 = jnp.bfloat16

D_MODEL = 1024
GRID_W = 64
M_HEADS = 4
M_DQK = 64
M_DV = 128
M_CHUNK = 128
M_CONV = 5
NA_HEADS = 4
NA_DH = 64
NA_KH = 8
NA_KW = 16
MEM_HEADS = 4
MEM_DH = 64
N_EXPERTS = 16
EC_CAPACITY = 2
D_FF = 2048
N_GATES = 4 * M_HEADS
EPS = 1e-6

LANES = 128
QK_W = 2 * M_HEADS * M_DQK
V_W = M_HEADS * M_DV
NA_W = NA_HEADS * NA_DH
MEM_W = MEM_HEADS * MEM_DH
SEG_QK = (0, QK_W)
SEG_V = (SEG_QK[1], SEG_QK[1] + V_W)
SEG_O = (SEG_V[1], SEG_V[1] + V_W)
SEG_G = (SEG_O[1], SEG_O[1] + LANES)
SEG_NA = (SEG_G[1], SEG_G[1] + 3 * NA_W)
SEG_CQ = (SEG_NA[1], SEG_NA[1] + MEM_W)
PROJ_W = SEG_CQ[1]
XZ_X = (0, D_MODEL)
XZ_H = (XZ_X[1], XZ_X[1] + D_MODEL)
XZ_A = (XZ_H[1], XZ_H[1] + LANES)
XZ_W = XZ_A[1]
NA_QROWS = 4
NA_KROWS = NA_QROWS + NA_KH
NEG_BIG = -1e30
VMEM_LIMIT = 56 * 1024 * 1024


def _cparams(n_axes):
    return pltpu.CompilerParams(dimension_semantics=("arbitrary",) * n_axes, vmem_limit_bytes=VMEM_LIMIT)


def _dot(a, b):
    return jnp.dot(a, b, preferred_element_type=F32)


def _dot_nt(a, b):
    return lax.dot_general(a, b, (((1,), (1,)), ((), ())), preferred_element_type=F32)


def _dot_exact_rhs(a, b_bf16, terms=3):
    return sum(_dot(part, b_bf16) for part in _split_bf16(a, terms))


def _split_bf16(w, terms):
    parts, r = [], w
    for _ in range(terms):
        p = r.astype(BF16)
        parts.append(p)
        r = r - p.astype(F32)
    return parts


def _head_mask(width, head_dim, h):
    lane = lax.broadcasted_iota(jnp.int32, (1, width), 1)
    return (lane // head_dim) == h


def _same_head_matrix(width, head_dim):
    r = lax.broadcasted_iota(jnp.int32, (width, width), 0) // head_dim
    c = lax.broadcasted_iota(jnp.int32, (width, width), 1) // head_dim
    return jnp.where(r == c, 1.0, 0.0).astype(BF16)


def _per_head_rms(x, gain, head_dim):
    ss = _dot_exact_rhs(x * x, _same_head_matrix(x.shape[1], head_dim), terms=2)
    return x * lax.rsqrt(ss * (1.0 / head_dim) + EPS) * gain


def _rms_rows(x, gain):
    ms = jnp.mean(x * x, axis=-1, keepdims=True)
    return x * lax.rsqrt(ms + EPS) * gain


def _softmax_rows(s):
    m = jnp.max(s, axis=-1, keepdims=True)
    e = jnp.exp(s - m)
    return e * (1.0 / jnp.sum(e, axis=-1, keepdims=True))


def _inproj_kernel(x_ref, g_ref, w_ref, na_gq_ref, na_gk_ref, mem_gq_ref, qk_ref, v_ref, o_ref, gt_ref, na_ref,
                   cq_ref):
    h = _rms_rows(x_ref[...], g_ref[...]).astype(BF16)
    for seg, out in ((SEG_QK, qk_ref), (SEG_V, v_ref), (SEG_O, o_ref), (SEG_G, gt_ref)):
        out[...] = _dot(h, w_ref[:, seg[0]:seg[1]])
    a = SEG_NA[0]
    nq = _dot(h, w_ref[:, a:a + NA_W])
    na_ref[:, :NA_W] = (_per_head_rms(nq, na_gq_ref[...], NA_DH) * (NA_DH ** -0.5)).astype(BF16)
    nk = _dot(h, w_ref[:, a + NA_W:a + 2 * NA_W])
    na_ref[:, NA_W:2 * NA_W] = _per_head_rms(nk, na_gk_ref[...], NA_DH).astype(BF16)
    na_ref[:, 2 * NA_W:] = _dot(h, w_ref[:, a + 2 * NA_W:a + 3 * NA_W]).astype(BF16)
    cq = _dot(h, w_ref[:, SEG_CQ[0]:SEG_CQ[1]])
    cq_ref[...] = (_per_head_rms(cq, mem_gq_ref[...], MEM_DH) * (MEM_DH ** -0.5)).astype(BF16)


def _inproj(x_src, g, w, na_gq, na_gk, mem_gq, layer, tm):
    n = x_src.shape[0]
    outs = [(SEG_QK, F32), (SEG_V, F32), (SEG_O, F32), (SEG_G, F32), (SEG_NA, BF16), (SEG_CQ, BF16)]
    per_layer_row = lambda w_: pl.BlockSpec((None, 1, w_), lambda i: (layer, 0, 0))
    return pl.pallas_call(
        _inproj_kernel,
        grid=(n // tm,),
        in_specs=[pl.BlockSpec((tm, D_MODEL), lambda i: (i, 0)),
                  per_layer_row(D_MODEL),
                  pl.BlockSpec((None, D_MODEL, PROJ_W), lambda i: (layer, 0, 0)),
                  per_layer_row(NA_W), per_layer_row(NA_W), per_layer_row(MEM_W)],
        out_specs=[pl.BlockSpec((tm, s[1] - s[0]), lambda i: (i, 0)) for s, _ in outs],
        out_shape=[jax.ShapeDtypeStruct((n, s[1] - s[0]), dt) for s, dt in outs],
        compiler_params=_cparams(1),
        name="in_proj",
    )(x_src, g, w, na_gq, na_gk, mem_gq)


HALO = 8


def _log_sigmoid(x):
    return -(jnp.maximum(-x, 0.0) + jnp.log(1.0 + jnp.exp(-jnp.abs(x))))


def _qk_conv_kernel(qk_ref, prev_ref, next_ref, cw_ref, q_ref, k_ref):
    i = pl.program_id(1)
    rows = qk_ref.shape[0]
    prev_on = jnp.where(i > 0, 1.0, 0.0)
    next_on = jnp.where(i < pl.num_programs(1) - 1, 1.0, 0.0)
    ext = jnp.concatenate([prev_ref[...] * prev_on, qk_ref[...], next_ref[...] * next_on], axis=0)
    pad = M_CONV // 2
    conv = ext[HALO - pad:HALO - pad + rows] * cw_ref[0:1, :]
    for j in range(1, M_CONV):
        conv = conv + ext[HALO - pad + j:HALO - pad + j + rows] * cw_ref[j:j + 1, :]
    qk = conv * jax.nn.sigmoid(conv)
    q_ref[...] = (qk[:, :QK_W // 2] * (M_DQK ** -0.5)).astype(BF16)
    k_ref[...] = qk[:, QK_W // 2:]


def _qk_conv(qk, conv_w, layer, bsz, seq, rows):
    hb = rows // HALO
    n_halo = seq // HALO
    return pl.pallas_call(
        _qk_conv_kernel,
        grid=(bsz, seq // rows),
        in_specs=[pl.BlockSpec((None, rows, QK_W), lambda b, i: (b, i, 0)),
                  pl.BlockSpec((None, HALO, QK_W), lambda b, i: (b, jnp.maximum(i * hb - 1, 0), 0)),
                  pl.BlockSpec((None, HALO, QK_W), lambda b, i: (b, jnp.minimum((i + 1) * hb, n_halo - 1), 0)),
                  pl.BlockSpec((None, M_CONV, QK_W), lambda b, i: (layer, 0, 0))],
        out_specs=[pl.BlockSpec((None, rows, QK_W // 2), lambda b, i: (b, i, 0))] * 2,
        out_shape=[jax.ShapeDtypeStruct((bsz, seq, QK_W // 2), BF16),
                   jax.ShapeDtypeStruct((bsz, seq, QK_W // 2), F32)],
        compiler_params=_cparams(2),
        name="mlstm_qk_conv",
    )(qk, qk, qk, conv_w)


def _lane_broadcast_columns(x, first, count):
    r = lax.broadcasted_iota(jnp.int32, (LANES, count * LANES), 0)
    c = lax.broadcasted_iota(jnp.int32, (LANES, count * LANES), 1)
    pick = jnp.where(r == first + c // LANES, 1.0, 0.0).astype(BF16)
    return _dot_exact_rhs(x, pick)


def _mlstm_direction(reverse, q_ref, k_ref, v_ref, g_ref, bg_ref, h_ref, c_ref, m_ref):
    L = M_CHUNK
    gates = g_ref[...] + bg_ref[...]
    log_f = _log_sigmoid(gates)
    t_i = lax.broadcasted_iota(jnp.int32, (L, L), 0)
    s_i = lax.broadcasted_iota(jnp.int32, (L, L), 1)
    visible = (s_i >= t_i) if reverse else (s_i <= t_i)
    b_all = _masked_prefix_sum(visible, log_f)
    off = 2 * M_HEADS if reverse else 0
    last = 0 if reverse else L - 1
    b_cols = _lane_broadcast_columns(b_all, off + M_HEADS, M_HEADS)
    li_cols = _lane_broadcast_columns(gates, off, M_HEADS)
    row_term_t = (gates - pltpu.roll(b_all, LANES - M_HEADS, 1)).T
    ones_b = jnp.ones((L, M_DV), BF16)
    heads_per_block = LANES // M_DQK

    for h in range(M_HEADS):
        blk = slice((h // heads_per_block) * LANES, (h // heads_per_block + 1) * LANES)
        hm = _head_mask(LANES, M_DQK, h % heads_per_block)
        hs = slice(h * LANES, (h + 1) * LANES)
        sidx = (M_HEADS if reverse else 0) + h
        b_col = b_cols[:, hs]
        m_prev = m_ref[sidx:sidx + 1, :]
        g_tot = b_col[last:last + 1, :]
        c_prev_b = c_ref[sidx].astype(BF16)
        q_hb = jnp.where(hm, q_ref[:, blk], jnp.zeros((), BF16))
        k_blk = k_ref[:, blk]
        v_hb = v_ref[:, h * M_DV:(h + 1) * M_DV].astype(BF16)

        d_log = jnp.where(visible, b_col + row_term_t[off + h:off + h + 1, :], -jnp.inf)
        inter = b_col + m_prev
        m_t = jnp.maximum(inter, jnp.max(d_log, axis=-1, keepdims=True))
        s_inter = jnp.exp(inter - m_t)
        scores = _dot_nt(q_hb, k_blk.astype(BF16)) * jnp.exp(d_log - m_t)
        qcn = _dot(q_hb, c_prev_b)
        num = s_inter * qcn[:, :M_DV] + _dot(scores.astype(BF16), v_hb)
        den = s_inter * qcn[:, M_DV:] + jnp.sum(scores, axis=-1, keepdims=True)
        h_ref[:, h * M_DV:(h + 1) * M_DV] = num / jnp.maximum(jnp.abs(den), jnp.exp(-m_t))

        w_log = g_tot - b_col + li_cols[:, hs]
        a = jnp.max(w_log, axis=0, keepdims=True)
        kw_t = (k_blk * jnp.exp(w_log - a)).T.astype(BF16)
        kvn = _dot(kw_t, jnp.concatenate([v_hb, ones_b], axis=1))
        m_new = jnp.maximum(g_tot + m_prev, a)
        s_prev = jnp.exp(g_tot + m_prev - m_new)
        s_cur = jnp.exp(a - m_new)
        two = lambda r: jnp.concatenate([r, r], axis=1)
        c_ref[sidx] = two(s_prev) * c_ref[sidx] + two(s_cur) * kvn
        m_ref[sidx:sidx + 1, :] = m_new


def _masked_prefix_sum(visible, x):
    tri = jnp.where(visible, 1.0, 0.0).astype(BF16)
    x0, x1, x2 = _split_bf16(x, 3)
    return _dot(tri, x0) + _dot(tri, x1) + _dot(tri, x2)


def _mlstm_kernel(qf, kf, vf, gf, qb, kb, vb, gb, bg_ref, hf_ref, hb_ref, c_ref, m_ref):
    @pl.when(pl.program_id(1) == 0)
    def _():
        c_ref[...] = jnp.zeros_like(c_ref)
        m_ref[...] = jnp.zeros_like(m_ref)

    _mlstm_direction(False, qf, kf, vf, gf, bg_ref, hf_ref, c_ref, m_ref)
    _mlstm_direction(True, qb, kb, vb, gb, bg_ref, hb_ref, c_ref, m_ref)


def _mlstm(qk, v, gt, conv_w, b_gate, layer, bsz, seq):
    L = M_CHUNK
    nc = seq // L
    q_b, k = _qk_conv(qk.reshape(bsz, seq, QK_W), conv_w, layer, bsz, seq, 4 * L)
    v3 = v.reshape(bsz, seq, V_W)
    g3 = gt.reshape(bsz, seq, LANES)

    def specs(cidx):
        chunk = lambda w_: pl.BlockSpec((None, L, w_), lambda b, c: (b, cidx(c), 0))
        return [chunk(QK_W // 2), chunk(QK_W // 2), chunk(V_W), chunk(LANES)]

    fw = lambda c: c
    bw = lambda c: nc - 1 - c
    h_fw, h_bw = pl.pallas_call(
        _mlstm_kernel,
        grid=(bsz, nc),
        in_specs=specs(fw) + specs(bw) + [pl.BlockSpec((None, 1, LANES), lambda b, c: (layer, 0, 0))],
        out_specs=[pl.BlockSpec((None, L, V_W), lambda b, c: (b, c, 0)),
                   pl.BlockSpec((None, L, V_W), lambda b, c: (b, nc - 1 - c, 0))],
        out_shape=[jax.ShapeDtypeStruct((bsz, seq, V_W), F32)] * 2,
        scratch_shapes=[pltpu.VMEM((2 * M_HEADS, LANES, 2 * M_DV), F32),
                        pltpu.VMEM((2 * M_HEADS, LANES), F32)],
        compiler_params=_cparams(2),
        name="mlstm",
    )(q_b, k, v3, g3, q_b, k, v3, g3, b_gate)
    return h_fw.reshape(bsz * seq, V_W), h_bw.reshape(bsz * seq, V_W)


def _na_kernel(q_ref, k0, k1, k2, v0, v1, v2, bias_ref, o_ref):
    q = q_ref[...]
    k = jnp.concatenate([k0[...], k1[...], k2[...]], axis=0)
    v = jnp.concatenate([v0[...], v1[...], v2[...]], axis=0)
    zero = jnp.zeros((), q.dtype)
    acc = jnp.zeros(o_ref.shape, F32)
    for h in range(NA_HEADS):
        hm = _head_mask(NA_W, NA_DH, h)
        s = _dot_nt(jnp.where(hm, q, zero), k) + bias_ref[h]
        p = _softmax_rows(s).astype(BF16)
        acc = acc + _dot(p, jnp.where(hm, v, zero))
    o_ref[...] = acc


def _na_bias_selectors(rows):
    nj = rows // NA_QROWS
    qi = np.arange(NA_QROWS)
    ki = np.arange(NA_KROWS)
    row_sel = np.zeros((3, NA_QROWS, NA_KROWS, 2 * NA_KH - 1), np.float32)
    row_ok = np.zeros((3, NA_QROWS, NA_KROWS), bool)
    for pat, j in enumerate((0, 1, nj - 1)):
        ks = min(max(NA_QROWS * j - NA_KH // 2, 0), rows - NA_KROWS)
        r = NA_QROWS * j + qi
        rs = np.clip(r - NA_KH // 2, 0, rows - NA_KH)
        krow = ks + ki
        ok = (krow[None, :] >= rs[:, None]) & (krow[None, :] < rs[:, None] + NA_KH)
        dr = krow[None, :] - r[:, None] + NA_KH - 1
        for a in range(NA_QROWS):
            for c in range(NA_KROWS):
                if ok[a, c]:
                    row_sel[pat, a, c, dr[a, c]] = 1.0
        row_ok[pat] = ok
    qc = np.arange(GRID_W)
    cs = np.clip(qc - NA_KW // 2, 0, GRID_W - NA_KW)
    col_ok = (qc[None, :] >= cs[:, None]) & (qc[None, :] < cs[:, None] + NA_KW)
    dc = qc[None, :] - qc[:, None] + NA_KW - 1
    col_sel = np.zeros((2 * NA_KW - 1, GRID_W, GRID_W), np.float32)
    for a in range(GRID_W):
        for c in range(GRID_W):
            if col_ok[a, c]:
                col_sel[dc[a, c], a, c] = 1.0
    ok = row_ok[:, :, None, :, None] & col_ok[None, None, :, None, :]
    return (row_sel.reshape(3 * NA_QROWS * NA_KROWS, -1), col_sel.reshape(2 * NA_KW - 1, -1),
            ok.reshape(3, NA_QROWS * GRID_W, NA_KROWS * GRID_W))


def _na_bias_table(rpb, rows):
    row_sel, col_sel, ok = _na_bias_selectors(rows)
    hi = lax.Precision.HIGHEST
    picked_rows = jnp.einsum('gr,hrd->hgd', row_sel, rpb, precision=hi)
    band = jnp.einsum('hgd,dn->hgn', picked_rows, col_sel, precision=hi)
    band = band.reshape(-1, 3, NA_QROWS, NA_KROWS, GRID_W, GRID_W).transpose(0, 1, 2, 4, 3, 5)
    return jnp.where(ok[None], band.reshape(-1, 3, NA_QROWS * GRID_W, NA_KROWS * GRID_W), NEG_BIG)


def _neighbourhood_attention(na, bias, layer, bsz, seq):
    rows = seq // GRID_W
    nj = rows // NA_QROWS
    qn = NA_QROWS * GRID_W
    kn = NA_KROWS * GRID_W
    na3 = na.reshape(bsz, seq, 3 * NA_W)
    kstart = lambda j: jnp.clip(j - 1, 0, nj - 3)
    pattern = lambda j: jnp.minimum(j, 1) + jnp.maximum(j - (nj - 2), 0)
    kv_specs = [pl.BlockSpec((None, qn, NA_W), functools.partial(lambda j, b, i, part: (b, kstart(j) + i, part),
                                                                 i=i, part=part))
                for part in (1, 2) for i in range(3)]
    out = pl.pallas_call(
        _na_kernel,
        grid=(nj, bsz),
        in_specs=[pl.BlockSpec((None, qn, NA_W), lambda j, b: (b, j, 0))] + kv_specs + [
            pl.BlockSpec((None, NA_HEADS, None, qn, kn), lambda j, b: (layer, 0, pattern(j), 0, 0))],
        out_specs=pl.BlockSpec((None, qn, NA_W), lambda j, b: (b, j, 0)),
        out_shape=jax.ShapeDtypeStruct((bsz, seq, NA_W), F32),
        compiler_params=_cparams(2),
        name="neighbourhood_attn",
    )(na3, na3, na3, na3, na3, na3, na3, bias)
    return out.reshape(bsz * seq, NA_W)


def _mem_kv_kernel(mem_ref, g_ref, w_ref, gk_ref, k_ref, v_ref):
    h = _rms_rows(mem_ref[...], g_ref[...]).astype(BF16)
    kv = _dot(h, w_ref[...])
    k_ref[...] = _per_head_rms(kv[:, :MEM_W], gk_ref[...], MEM_DH).astype(BF16)
    v_ref[...] = kv[:, MEM_W:].astype(BF16)


def _mem_kv(mem, g_mem, w_kv, gk, layer):
    bsz, n_mem, _ = mem.shape
    return pl.pallas_call(
        _mem_kv_kernel,
        grid=(bsz,),
        in_specs=[pl.BlockSpec((None, n_mem, D_MODEL), lambda b: (b, 0, 0)),
                  pl.BlockSpec((None, 1, D_MODEL), lambda b: (layer, 0, 0)),
                  pl.BlockSpec((None, D_MODEL, 2 * MEM_W), lambda b: (layer, 0, 0)),
                  pl.BlockSpec((None, 1, MEM_W), lambda b: (layer, 0, 0))],
        out_specs=[pl.BlockSpec((None, n_mem, MEM_W), lambda b: (b, 0, 0))] * 2,
        out_shape=[jax.ShapeDtypeStruct((bsz, n_mem, MEM_W), BF16)] * 2,
        compiler_params=_cparams(1),
        name="mem_kv",
    )(mem, g_mem, w_kv, gk)


def _mem_attn_kernel(q_ref, k_ref, v_ref, o_ref):
    q = q_ref[...]
    k_b = k_ref[...]
    v_b = v_ref[...]
    zero = jnp.zeros((), q.dtype)
    acc = jnp.zeros(o_ref.shape, F32)
    for h in range(MEM_HEADS):
        hm = _head_mask(MEM_W, MEM_DH, h)
        p = _softmax_rows(_dot_nt(jnp.where(hm, q, zero), k_b)).astype(BF16)
        acc = acc + _dot(p, jnp.where(hm, v_b, zero))
    o_ref[...] = acc


def _mem_attention(cq, k_m, v_m, bsz, seq, tq):
    n_mem = k_m.shape[1]
    out = pl.pallas_call(
        _mem_attn_kernel,
        grid=(bsz, seq // tq),
        in_specs=[pl.BlockSpec((None, tq, MEM_W), lambda b, i: (b, i, 0)),
                  pl.BlockSpec((None, n_mem, MEM_W), lambda b, i: (b, 0, 0)),
                  pl.BlockSpec((None, n_mem, MEM_W), lambda b, i: (b, 0, 0))],
        out_specs=pl.BlockSpec((None, tq, MEM_W), lambda b, i: (b, i, 0)),
        out_shape=jax.ShapeDtypeStruct((bsz, seq, MEM_W), F32),
        compiler_params=_cparams(2),
        name="mem_attn",
    )(cq.reshape(bsz, seq, MEM_W), k_m, v_m)
    return out.reshape(bsz * seq, MEM_W)


def _outproj_kernel(hf_ref, hb_ref, o_ref, yn_ref, yc_ref, x_ref, w_ref, gh_ref, gf_ref, wr_ref, xz_ref, afft_ref):
    hs = hf_ref[...] + hb_ref[...]
    acc = x_ref[...]
    for h in range(M_HEADS):
        sl = slice(h * M_DV, (h + 1) * M_DV)
        y = jax.nn.sigmoid(o_ref[:, sl]) * _rms_rows(hs[:, sl], gh_ref[:, sl])
        acc = acc + _dot(y.astype(BF16), w_ref[sl, :])
    acc = acc + _dot(yn_ref[...].astype(BF16), w_ref[V_W:V_W + NA_W, :])
    acc = acc + _dot(yc_ref[...].astype(BF16), w_ref[V_W + NA_W:, :])
    xz_ref[:, XZ_X[0]:XZ_X[1]] = acc
    h2 = _rms_rows(acc, gf_ref[...])
    xz_ref[:, XZ_H[0]:XZ_H[1]] = h2
    h2_hi, h2_lo = _split_bf16(h2, 2)
    part = _dot(h2_hi, wr_ref[...]) + _dot(h2_lo, wr_ref[...])
    logits = part + pltpu.roll(part, LANES - N_EXPERTS, 1) + pltpu.roll(part, LANES - 2 * N_EXPERTS, 1)
    lane = lax.broadcasted_iota(jnp.int32, logits.shape, 1)
    aff = _softmax_rows(jnp.where(lane < N_EXPERTS, logits, NEG_BIG))
    xz_ref[:, XZ_A[0]:XZ_A[1]] = aff
    afft_ref[...] = aff.T[:N_EXPERTS, :]


def _outproj(h_fw, h_bw, o, y_n, y_c, x_src, w_out, g_head, g_ffn, w_router_p, layer, tm):
    n = x_src.shape[0]
    row = lambda w_: pl.BlockSpec((tm, w_), lambda i: (i, 0))
    per_layer = lambda a: pl.BlockSpec((None,) + a.shape[1:], lambda i: (layer, 0, 0))
    return pl.pallas_call(
        _outproj_kernel,
        grid=(n // tm,),
        in_specs=[row(V_W), row(V_W), row(V_W), row(NA_W), row(MEM_W), row(D_MODEL), per_layer(w_out),
                  per_layer(g_head), per_layer(g_ffn), per_layer(w_router_p)],
        out_specs=[row(XZ_W), pl.BlockSpec((N_EXPERTS, tm), lambda i: (0, i))],
        out_shape=[jax.ShapeDtypeStruct((n, XZ_W), F32), jax.ShapeDtypeStruct((N_EXPERTS, n), F32)],
        compiler_params=_cparams(1),
        name="out_proj_router",
    )(h_fw, h_bw, o, y_n, y_c, x_src, w_out, g_head, g_ffn, w_router_p)


def _lane_cumsum(mask_f, tri_b):
    n = mask_f.shape[1]
    run = jnp.zeros((mask_f.shape[0], 1), F32)
    parts = []
    for j in range(n // LANES):
        local = _dot(mask_f[:, j * LANES:(j + 1) * LANES].astype(BF16), tri_b) + run
        parts.append(local)
        run = local[:, LANES - 1:LANES]
    return jnp.concatenate(parts, axis=1)


def _select_kernel(aff_ref, idx_ref, csum_ref, *, cap):
    bits = pltpu.bitcast(aff_ref[...], jnp.int32)
    n_e, n_tok = bits.shape
    thr = jnp.zeros((n_e, 1), jnp.int32)
    for bit in range(30, -1, -1):
        cand = thr | (1 << bit)
        cnt = jnp.sum(jnp.where(bits >= cand, 1.0, 0.0), axis=1, keepdims=True)
        thr = jnp.where(cnt >= cap, cand, thr)
    s_i = lax.broadcasted_iota(jnp.int32, (LANES, LANES), 0)
    t_i = lax.broadcasted_iota(jnp.int32, (LANES, LANES), 1)
    tri_b = jnp.where(s_i <= t_i, 1.0, 0.0).astype(BF16)
    gt = bits > thr
    eq_f = jnp.where(bits == thr, 1.0, 0.0)
    need = cap - jnp.sum(jnp.where(gt, 1.0, 0.0), axis=1, keepdims=True)
    eq_rank = _lane_cumsum(eq_f, tri_b)
    sel_f = jnp.where(gt | ((eq_f > 0.5) & (eq_rank <= need)), 1.0, 0.0)
    csum = _lane_cumsum(sel_f, tri_b)
    for e in range(n_e):
        csum_ref[e] = csum[e:e + 1, :]
    slot = lax.broadcasted_iota(jnp.int32, (cap, 1), 0).astype(F32)
    lane = lax.broadcasted_iota(jnp.int32, (cap, LANES), 1)

    def per_expert(e, cols):
        acc = jnp.zeros((cap, LANES), F32)
        for j in range(n_tok // LANES):
            acc = acc + jnp.where(csum_ref[e, :, j * LANES:(j + 1) * LANES] <= slot, 1.0, 0.0)
        return jnp.where(lane == e, jnp.sum(acc, axis=1, keepdims=True), cols)

    cols = lax.fori_loop(0, n_e, per_expert, jnp.zeros((cap, LANES), F32))
    idx_ref[...] = cols.T[:n_e, :].astype(jnp.int32)


def _select(aff_t, bsz, seq, cap):
    return pl.pallas_call(
        functools.partial(_select_kernel, cap=cap),
        grid=(bsz,),
        in_specs=[pl.BlockSpec((N_EXPERTS, seq), lambda b: (0, b))],
        out_specs=pl.BlockSpec((None, N_EXPERTS, cap), lambda b: (b, 0, 0)),
        out_shape=jax.ShapeDtypeStruct((bsz, N_EXPERTS, cap), jnp.int32),
        scratch_shapes=[pltpu.VMEM((N_EXPERTS, 1, seq), F32)],
        compiler_params=_cparams(1),
        name="expert_select",
    )(aff_t)


FF_CHUNK = 512


N_ROWBUF = 3


def _experts_kernel(idx_prev, idx_cur, idx_next, xz_hbm, w1_ref, w3_ref, w2_ref, out_hbm, buf, xb_ref, gsem, ssem, *,
                    cap, seq):
    del xz_hbm
    e, b = pl.program_id(0), pl.program_id(1)
    n_e, n_b = pl.num_programs(0), pl.num_programs(1)
    k = e * n_b + b
    slot_cur = lax.rem(k, N_ROWBUF)
    slot_next = lax.rem(k + 1, N_ROWBUF)
    slot_prev = lax.rem(k + 2, N_ROWBUF)
    base_cur = b * seq
    base_next = jnp.where(b + 1 == n_b, 0, b + 1) * seq
    base_prev = jnp.where(b == 0, n_b - 1, b - 1) * seq

    def fetch_row(idx_ref, base, slot, i):
        return pltpu.make_async_copy(out_hbm.at[pl.ds(base + idx_ref[0, i], 1)], buf.at[slot, pl.ds(i, 1)],
                                     gsem.at[slot])

    def writeback_row(idx_ref, base, slot, i):
        return pltpu.make_async_copy(buf.at[slot, pl.ds(i, 1), pl.ds(XZ_X[0], D_MODEL)],
                                     out_hbm.at[pl.ds(base + idx_ref[0, i], 1), pl.ds(XZ_X[0], D_MODEL)],
                                     ssem.at[slot])

    def wait_fetched(slot):
        pltpu.make_async_copy(out_hbm.at[pl.ds(0, cap)], buf.at[slot], gsem.at[slot]).wait()

    def wait_written(slot):
        pltpu.make_async_copy(buf.at[slot, :, pl.ds(XZ_X[0], D_MODEL)],
                              out_hbm.at[pl.ds(0, cap), pl.ds(XZ_X[0], D_MODEL)], ssem.at[slot]).wait()

    def start_all(make_copy):
        def body(i, carry):
            make_copy(i).start()
            return carry
        lax.fori_loop(0, cap, body, 0)

    @pl.when(k == 0)
    def _():
        start_all(lambda i: fetch_row(idx_prev, base_prev, slot_prev, i))
        wait_fetched(slot_prev)
        start_all(lambda i: fetch_row(idx_cur, base_cur, slot_cur, i))

    wait_fetched(slot_cur)
    lane = lax.broadcasted_iota(jnp.int32, (cap, LANES), 1)
    gate = jnp.sum(jnp.where(lane == e, buf[slot_cur, :, XZ_A[0]:XZ_A[1]], 0.0), axis=1, keepdims=True)
    n_ff = D_FF // FF_CHUNK
    per_chunk = 2 * cap // n_ff
    xb_ref[...] = buf[slot_cur, :, XZ_H[0]:XZ_H[1]].astype(BF16)
    for f in range(n_ff):
        xb = xb_ref[...]
        for i in range(f * per_chunk, (f + 1) * per_chunk):
            if i < cap:
                fetch_row(idx_next, base_next, slot_next, i).start()
            else:
                writeback_row(idx_prev, base_prev, slot_prev, i - cap).start()
        sl = slice(f * FF_CHUNK, (f + 1) * FF_CHUNK)
        h1 = _dot(xb, w1_ref[:, sl])
        h3 = _dot(xb, w3_ref[:, sl])
        hid = (h1 * jax.nn.sigmoid(h1) * h3).astype(BF16)
        buf[slot_cur, :, XZ_X[0]:XZ_X[1]] = buf[slot_cur, :, XZ_X[0]:XZ_X[1]] + _dot(hid, w2_ref[sl, :]) * gate
    wait_written(slot_prev)

    @pl.when(k == n_e * n_b - 1)
    def _():
        wait_fetched(slot_next)
        start_all(lambda i: writeback_row(idx_cur, base_cur, slot_cur, i))
        wait_written(slot_cur)


def _experts(idx, xz, w1, w3, w2, layer, bsz, seq, cap):
    assert bsz >= N_ROWBUF, "rows in flight for neighbouring steps must belong to different sequences"
    idx3 = idx.reshape(bsz * N_EXPERTS, 1, cap)

    def idx_spec(step):
        def index_map(e, b):
            b2 = lax.rem(b + step + bsz, bsz)
            e2 = jnp.clip(e + (b + step + bsz) // bsz - 1, 0, N_EXPERTS - 1)
            return (b2 * N_EXPERTS + e2, 0, 0)
        return pl.BlockSpec((None, 1, cap), index_map, memory_space=pltpu.SMEM)

    w_up = pl.BlockSpec((None, None, D_MODEL, D_FF), lambda e, b: (layer, e, 0, 0))
    w_down = pl.BlockSpec((None, None, D_FF, D_MODEL), lambda e, b: (layer, e, 0, 0))
    return pl.pallas_call(
        functools.partial(_experts_kernel, cap=cap, seq=seq),
        grid=(N_EXPERTS, bsz),
        in_specs=[idx_spec(-1), idx_spec(0), idx_spec(1), pl.BlockSpec(memory_space=pl.ANY), w_up, w_up, w_down],
        out_specs=pl.BlockSpec(memory_space=pl.ANY),
        out_shape=jax.ShapeDtypeStruct(xz.shape, F32),
        scratch_shapes=[pltpu.VMEM((N_ROWBUF, cap, XZ_W), F32), pltpu.VMEM((cap, D_MODEL), BF16),
                        pltpu.SemaphoreType.DMA((N_ROWBUF,)),
                        pltpu.SemaphoreType.DMA((N_ROWBUF,))],
        input_output_aliases={3: 0},
        compiler_params=_cparams(2),
        name="experts",
    )(idx3, idx3, idx3, xz, w1, w3, w2)


def _repack_w_in(w_in):
    mlstm_w = 2 * M_HEADS * M_DQK + 2 * M_HEADS * M_DV
    pad = jnp.zeros(w_in.shape[:2] + (LANES - N_GATES,), w_in.dtype)
    return jnp.concatenate([w_in[..., :mlstm_w + N_GATES], pad, w_in[..., mlstm_w + N_GATES:]], axis=-1).astype(BF16)


def kernel(x, mem, g_mix, w_in, b_gates, conv_qk, g_mlstm_head, na_gq, na_gk, na_rpb, g_mem, w_mem_kv, mem_gq,
           mem_gk, w_out, g_ffn, w_router, w1, w3, w2):
    bsz, seq, _ = x.shape
    depth = w_in.shape[0]
    rows = seq // GRID_W
    cap = EC_CAPACITY * seq // N_EXPERTS
    tm = 512

    w_in_b = _repack_w_in(w_in)
    w_out_b = w_out.astype(BF16)
    w_kv_b = w_mem_kv.astype(BF16)
    w1_b, w3_b, w2_b = w1.astype(BF16), w3.astype(BF16), w2.astype(BF16)
    b_gate_p = jnp.pad(b_gates, ((0, 0), (0, LANES - N_GATES)))[:, None, :]
    w_router_p = jnp.pad(jnp.concatenate(_split_bf16(w_router, 3), axis=-1),
                         ((0, 0), (0, 0), (0, LANES - 3 * N_EXPERTS)))
    na_bias = jax.vmap(lambda r: _na_bias_table(r, rows))(na_rpb)
    row3 = lambda a: a[:, None, :]
    per_head = lambda a, n_heads: row3(jnp.tile(a, (1, n_heads)))
    g_mix3, g_mem3, g_head3, g_ffn3 = row3(g_mix), row3(g_mem), row3(g_mlstm_head), row3(g_ffn)
    na_gq3, na_gk3 = per_head(na_gq, NA_HEADS), per_head(na_gk, NA_HEADS)
    mem_gq3, mem_gk3 = per_head(mem_gq, MEM_HEADS), per_head(mem_gk, MEM_HEADS)

    xs = x.reshape(bsz * seq, D_MODEL)
    for l in range(depth):
        qk, v, o, gt, na, cq = _inproj(xs, g_mix3, w_in_b, na_gq3, na_gk3, mem_gq3, l, tm)
        h_fw, h_bw = _mlstm(qk, v, gt, conv_qk, b_gate_p, l, bsz, seq)
        y_n = _neighbourhood_attention(na, na_bias, l, bsz, seq)
        k_m, v_m = _mem_kv(mem, g_mem3, w_kv_b, mem_gk3, l)
        y_c = _mem_attention(cq, k_m, v_m, bsz, seq, tm)
        xz, aff_t = _outproj(h_fw, h_bw, o, y_n, y_c, xs, w_out_b, g_head3, g_ffn3, w_router_p, l, tm)
        idx = _select(aff_t, bsz, seq, cap)
        xs = _experts(idx, xz, w1_b, w3_b, w2_b, l, bsz, seq, cap)
    return xs[:, XZ_X[0]:XZ_X[1]].reshape(bsz, seq, D_MODEL)
```

```python
import functools

import numpy as np
import jax
import jax.numpy as jnp
from jax import lax
from jax.experimental import pallas as pl
from jax.experimental.pallas import tpu as pltpu

F32 = jnp.float32
BF16 = jnp.bfloat16

D_MODEL = 1024
GRID_W = 64
M_HEADS = 4
M_DQK = 64
M_DV = 128
M_CHUNK = 128
M_CONV = 5
NA_HEADS = 4
NA_DH = 64
NA_KH = 8
NA_KW = 16
MEM_HEADS = 4
MEM_DH = 64
N_EXPERTS = 16
EC_CAPACITY = 2
D_FF = 2048
N_GATES = 4 * M_HEADS
EPS = 1e-6

LANES = 128
QK_W = 2 * M_HEADS * M_DQK
V_W = M_HEADS * M_DV
NA_W = NA_HEADS * NA_DH
MEM_W = MEM_HEADS * MEM_DH
SEG_QK = (0, QK_W)
SEG_V = (SEG_QK[1], SEG_QK[1] + V_W)
SEG_O = (SEG_V[1], SEG_V[1] + V_W)
SEG_G = (SEG_O[1], SEG_O[1] + LANES)
SEG_NA = (SEG_G[1], SEG_G[1] + 3 * NA_W)
SEG_CQ = (SEG_NA[1], SEG_NA[1] + MEM_W)
PROJ_W = SEG_CQ[1]
XZ_X = (0, D_MODEL)
XZ_H = (XZ_X[1], XZ_X[1] + D_MODEL)
XZ_A = (XZ_H[1], XZ_H[1] + LANES)
XZ_W = XZ_A[1]
NA_QROWS = 4
NA_KROWS = NA_QROWS + NA_KH
NEG_BIG = -1e30
VMEM_LIMIT = 56 * 1024 * 1024
IN_PROJ_ROW_GROUPS = 1
OUT_PROJ_ROW_GROUPS = 2


def _cparams(n_axes):
    return pltpu.CompilerParams(dimension_semantics=("arbitrary",) * n_axes, vmem_limit_bytes=VMEM_LIMIT)


def _dot(a, b):
    return jnp.dot(a, b, preferred_element_type=F32)


def _dot_nt(a, b):
    return lax.dot_general(a, b, (((1,), (1,)), ((), ())), preferred_element_type=F32)


def _dot_exact_rhs(a, b_bf16, terms=3):
    return sum(_dot(part, b_bf16) for part in _split_bf16(a, terms))


def _split_bf16(w, terms):
    parts, r = [], w
    for _ in range(terms):
        p = r.astype(BF16)
        parts.append(p)
        r = r - p.astype(F32)
    return parts


def _head_mask(width, head_dim, h):
    lane = lax.broadcasted_iota(jnp.int32, (1, width), 1)
    return (lane // head_dim) == h


def _same_head_matrix(width, head_dim):
    r = lax.broadcasted_iota(jnp.int32, (width, width), 0) // head_dim
    c = lax.broadcasted_iota(jnp.int32, (width, width), 1) // head_dim
    return jnp.where(r == c, 1.0, 0.0).astype(BF16)


def _per_head_rms(x, gain, head_dim):
    ss = _dot_exact_rhs(x * x, _same_head_matrix(x.shape[1], head_dim), terms=2)
    return x * lax.rsqrt(ss * (1.0 / head_dim) + EPS) * gain


def _rms_rows(x, gain):
    ms = jnp.mean(x * x, axis=-1, keepdims=True)
    return x * lax.rsqrt(ms + EPS) * gain


def _row_groups(n_rows, groups):
    step = n_rows // groups
    return [slice(r * step, (r + 1) * step) for r in range(groups)]


def _softmax_rows(s):
    m = jnp.max(s, axis=-1, keepdims=True)
    e = jnp.exp(s - m)
    return e * (1.0 / jnp.sum(e, axis=-1, keepdims=True))


def _inproj_kernel(x_ref, g_ref, w_ref, na_gq_ref, na_gk_ref, mem_gq_ref, qk_ref, v_ref, o_ref, gt_ref, na_ref,
                   cq_ref):
    for rows in _row_groups(x_ref.shape[0], IN_PROJ_ROW_GROUPS):
        h = _rms_rows(x_ref[rows, :], g_ref[...]).astype(BF16)
        for seg, out in ((SEG_QK, qk_ref), (SEG_V, v_ref), (SEG_O, o_ref), (SEG_G, gt_ref)):
            out[rows, :] = _dot(h, w_ref[:, seg[0]:seg[1]])
        a = SEG_NA[0]
        nq = _dot(h, w_ref[:, a:a + NA_W])
        na_ref[rows, :NA_W] = (_per_head_rms(nq, na_gq_ref[...], NA_DH) * (NA_DH ** -0.5)).astype(BF16)
        nk = _dot(h, w_ref[:, a + NA_W:a + 2 * NA_W])
        na_ref[rows, NA_W:2 * NA_W] = _per_head_rms(nk, na_gk_ref[...], NA_DH).astype(BF16)
        na_ref[rows, 2 * NA_W:] = _dot(h, w_ref[:, a + 2 * NA_W:a + 3 * NA_W]).astype(BF16)
        cq = _dot(h, w_ref[:, SEG_CQ[0]:SEG_CQ[1]])
        cq_ref[rows, :] = (_per_head_rms(cq, mem_gq_ref[...], MEM_DH) * (MEM_DH ** -0.5)).astype(BF16)


def _inproj(x_src, g, w, na_gq, na_gk, mem_gq, layer, tm):
    n = x_src.shape[0]
    outs = [(SEG_QK, F32), (SEG_V, F32), (SEG_O, F32), (SEG_G, F32), (SEG_NA, BF16), (SEG_CQ, BF16)]
    per_layer_row = lambda w_: pl.BlockSpec((None, 1, w_), lambda i: (layer, 0, 0))
    return pl.pallas_call(
        _inproj_kernel,
        grid=(n // tm,),
        in_specs=[pl.BlockSpec((tm, D_MODEL), lambda i: (i, 0)),
                  per_layer_row(D_MODEL),
                  pl.BlockSpec((None, D_MODEL, PROJ_W), lambda i: (layer, 0, 0)),
                  per_layer_row(NA_W), per_layer_row(NA_W), per_layer_row(MEM_W)],
        out_specs=[pl.BlockSpec((tm, s[1] - s[0]), lambda i: (i, 0)) for s, _ in outs],
        out_shape=[jax.ShapeDtypeStruct((n, s[1] - s[0]), dt) for s, dt in outs],
        compiler_params=_cparams(1),
        name="in_proj",
    )(x_src, g, w, na_gq, na_gk, mem_gq)


HALO = 8


def _log_sigmoid(x):
    return -(jnp.maximum(-x, 0.0) + jnp.log(1.0 + jnp.exp(-jnp.abs(x))))


def _qk_conv_kernel(qk_ref, prev_ref, next_ref, cw_ref, q_ref, k_ref):
    i = pl.program_id(1)
    rows = qk_ref.shape[0]
    prev_on = jnp.where(i > 0, 1.0, 0.0)
    next_on = jnp.where(i < pl.num_programs(1) - 1, 1.0, 0.0)
    ext = jnp.concatenate([prev_ref[...] * prev_on, qk_ref[...], next_ref[...] * next_on], axis=0)
    pad = M_CONV // 2
    conv = ext[HALO - pad:HALO - pad + rows] * cw_ref[0:1, :]
    for j in range(1, M_CONV):
        conv = conv + ext[HALO - pad + j:HALO - pad + j + rows] * cw_ref[j:j + 1, :]
    qk = conv * jax.nn.sigmoid(conv)
    q_ref[...] = (qk[:, :QK_W // 2] * (M_DQK ** -0.5)).astype(BF16)
    k_ref[...] = qk[:, QK_W // 2:]


def _qk_conv(qk, conv_w, layer, bsz, seq, rows):
    hb = rows // HALO
    n_halo = seq // HALO
    return pl.pallas_call(
        _qk_conv_kernel,
        grid=(bsz, seq // rows),
        in_specs=[pl.BlockSpec((None, rows, QK_W), lambda b, i: (b, i, 0)),
                  pl.BlockSpec((None, HALO, QK_W), lambda b, i: (b, jnp.maximum(i * hb - 1, 0), 0)),
                  pl.BlockSpec((None, HALO, QK_W), lambda b, i: (b, jnp.minimum((i + 1) * hb, n_halo - 1), 0)),
                  pl.BlockSpec((None, M_CONV, QK_W), lambda b, i: (layer, 0, 0))],
        out_specs=[pl.BlockSpec((None, rows, QK_W // 2), lambda b, i: (b, i, 0))] * 2,
        out_shape=[jax.ShapeDtypeStruct((bsz, seq, QK_W // 2), BF16),
                   jax.ShapeDtypeStruct((bsz, seq, QK_W // 2), F32)],
        compiler_params=_cparams(2),
        name="mlstm_qk_conv",
    )(qk, qk, qk, conv_w)


def _lane_broadcast_columns(x, first, count):
    r = lax.broadcasted_iota(jnp.int32, (LANES, count * LANES), 0)
    c = lax.broadcasted_iota(jnp.int32, (LANES, count * LANES), 1)
    pick = jnp.where(r == first + c // LANES, 1.0, 0.0).astype(BF16)
    return _dot_exact_rhs(x, pick)


def _mlstm_direction(reverse, q_ref, k_ref, v_ref, g_ref, bg_ref, h_ref, c_ref, m_ref):
    L = M_CHUNK
    gates = g_ref[...] + bg_ref[...]
    log_f = _log_sigmoid(gates)
    t_i = lax.broadcasted_iota(jnp.int32, (L, L), 0)
    s_i = lax.broadcasted_iota(jnp.int32, (L, L), 1)
    visible = (s_i >= t_i) if reverse else (s_i <= t_i)
    b_all = _masked_prefix_sum(visible, log_f)
    off = 2 * M_HEADS if reverse else 0
    last = 0 if reverse else L - 1
    b_cols = _lane_broadcast_columns(b_all, off + M_HEADS, M_HEADS)
    li_cols = _lane_broadcast_columns(gates, off, M_HEADS)
    row_term_t = (gates - pltpu.roll(b_all, LANES - M_HEADS, 1)).T
    ones_b = jnp.ones((L, M_DV), BF16)
    heads_per_block = LANES // M_DQK

    for h in range(M_HEADS):
        blk = slice((h // heads_per_block) * LANES, (h // heads_per_block + 1) * LANES)
        hm = _head_mask(LANES, M_DQK, h % heads_per_block)
        hs = slice(h * LANES, (h + 1) * LANES)
        sidx = (M_HEADS if reverse else 0) + h
        b_col = b_cols[:, hs]
        m_prev = m_ref[sidx:sidx + 1, :]
        g_tot = b_col[last:last + 1, :]
        c_prev_b = c_ref[sidx].astype(BF16)
        q_hb = jnp.where(hm, q_ref[:, blk], jnp.zeros((), BF16))
        k_blk = k_ref[:, blk]
        v_hb = v_ref[:, h * M_DV:(h + 1) * M_DV].astype(BF16)

        d_log = jnp.where(visible, b_col + row_term_t[off + h:off + h + 1, :], -jnp.inf)
        inter = b_col + m_prev
        m_t = jnp.maximum(inter, jnp.max(d_log, axis=-1, keepdims=True))
        s_inter = jnp.exp(inter - m_t)
        scores = _dot_nt(q_hb, k_blk.astype(BF16)) * jnp.exp(d_log - m_t)
        qcn = _dot(q_hb, c_prev_b)
        num = s_inter * qcn[:, :M_DV] + _dot(scores.astype(BF16), v_hb)
        den = s_inter * qcn[:, M_DV:] + jnp.sum(scores, axis=-1, keepdims=True)
        h_ref[:, h * M_DV:(h + 1) * M_DV] = num / jnp.maximum(jnp.abs(den), jnp.exp(-m_t))

        w_log = g_tot - b_col + li_cols[:, hs]
        a = jnp.max(w_log, axis=0, keepdims=True)
        kw_t = (k_blk * jnp.exp(w_log - a)).T.astype(BF16)
        kvn = _dot(kw_t, jnp.concatenate([v_hb, ones_b], axis=1))
        m_new = jnp.maximum(g_tot + m_prev, a)
        s_prev = jnp.exp(g_tot + m_prev - m_new)
        s_cur = jnp.exp(a - m_new)
        two = lambda r: jnp.concatenate([r, r], axis=1)
        c_ref[sidx] = two(s_prev) * c_ref[sidx] + two(s_cur) * kvn
        m_ref[sidx:sidx + 1, :] = m_new


def _masked_prefix_sum(visible, x):
    tri = jnp.where(visible, 1.0, 0.0).astype(BF16)
    x0, x1, x2 = _split_bf16(x, 3)
    return _dot(tri, x0) + _dot(tri, x1) + _dot(tri, x2)


def _mlstm_kernel(qf, kf, vf, gf, qb, kb, vb, gb, bg_ref, hf_ref, hb_ref, c_ref, m_ref):
    @pl.when(pl.program_id(1) == 0)
    def _():
        c_ref[...] = jnp.zeros_like(c_ref)
        m_ref[...] = jnp.zeros_like(m_ref)

    _mlstm_direction(False, qf, kf, vf, gf, bg_ref, hf_ref, c_ref, m_ref)
    _mlstm_direction(True, qb, kb, vb, gb, bg_ref, hb_ref, c_ref, m_ref)


def _mlstm(qk, v, gt, conv_w, b_gate, layer, bsz, seq):
    L = M_CHUNK
    nc = seq // L
    q_b, k = _qk_conv(qk.reshape(bsz, seq, QK_W), conv_w, layer, bsz, seq, 4 * L)
    v3 = v.reshape(bsz, seq, V_W)
    g3 = gt.reshape(bsz, seq, LANES)

    def specs(cidx):
        chunk = lambda w_: pl.BlockSpec((None, L, w_), lambda b, c: (b, cidx(c), 0))
        return [chunk(QK_W // 2), chunk(QK_W // 2), chunk(V_W), chunk(LANES)]

    fw = lambda c: c
    bw = lambda c: nc - 1 - c
    h_fw, h_bw = pl.pallas_call(
        _mlstm_kernel,
        grid=(bsz, nc),
        in_specs=specs(fw) + specs(bw) + [pl.BlockSpec((None, 1, LANES), lambda b, c: (layer, 0, 0))],
        out_specs=[pl.BlockSpec((None, L, V_W), lambda b, c: (b, c, 0)),
                   pl.BlockSpec((None, L, V_W), lambda b, c: (b, nc - 1 - c, 0))],
        out_shape=[jax.ShapeDtypeStruct((bsz, seq, V_W), F32)] * 2,
        scratch_shapes=[pltpu.VMEM((2 * M_HEADS, LANES, 2 * M_DV), F32),
                        pltpu.VMEM((2 * M_HEADS, LANES), F32)],
        compiler_params=_cparams(2),
        name="mlstm",
    )(q_b, k, v3, g3, q_b, k, v3, g3, b_gate)
    return h_fw.reshape(bsz * seq, V_W), h_bw.reshape(bsz * seq, V_W)


def _na_kernel(q_ref, k0, k1, k2, v0, v1, v2, bias_ref, o_ref):
    q = q_ref[...]
    k = jnp.concatenate([k0[...], k1[...], k2[...]], axis=0)
    v = jnp.concatenate([v0[...], v1[...], v2[...]], axis=0)
    zero = jnp.zeros((), q.dtype)
    acc = jnp.zeros(o_ref.shape, F32)
    for h in range(NA_HEADS):
        hm = _head_mask(NA_W, NA_DH, h)
        s = _dot_nt(jnp.where(hm, q, zero), k) + bias_ref[h]
        p = _softmax_rows(s).astype(BF16)
        acc = acc + _dot(p, jnp.where(hm, v, zero))
    o_ref[...] = acc


def _na_bias_selectors(rows):
    nj = rows // NA_QROWS
    qi = np.arange(NA_QROWS)
    ki = np.arange(NA_KROWS)
    row_sel = np.zeros((3, NA_QROWS, NA_KROWS, 2 * NA_KH - 1), np.float32)
    row_ok = np.zeros((3, NA_QROWS, NA_KROWS), bool)
    for pat, j in enumerate((0, 1, nj - 1)):
        ks = min(max(NA_QROWS * j - NA_KH // 2, 0), rows - NA_KROWS)
        r = NA_QROWS * j + qi
        rs = np.clip(r - NA_KH // 2, 0, rows - NA_KH)
        krow = ks + ki
        ok = (krow[None, :] >= rs[:, None]) & (krow[None, :] < rs[:, None] + NA_KH)
        dr = krow[None, :] - r[:, None] + NA_KH - 1
        for a in range(NA_QROWS):
            for c in range(NA_KROWS):
                if ok[a, c]:
                    row_sel[pat, a, c, dr[a, c]] = 1.0
        row_ok[pat] = ok
    qc = np.arange(GRID_W)
    cs = np.clip(qc - NA_KW // 2, 0, GRID_W - NA_KW)
    col_ok = (qc[None, :] >= cs[:, None]) & (qc[None, :] < cs[:, None] + NA_KW)
    dc = qc[None, :] - qc[:, None] + NA_KW - 1
    col_sel = np.zeros((2 * NA_KW - 1, GRID_W, GRID_W), np.float32)
    for a in range(GRID_W):
        for c in range(GRID_W):
            if col_ok[a, c]:
                col_sel[dc[a, c], a, c] = 1.0
    ok = row_ok[:, :, None, :, None] & col_ok[None, None, :, None, :]
    return (row_sel.reshape(3 * NA_QROWS * NA_KROWS, -1), col_sel.reshape(2 * NA_KW - 1, -1),
            ok.reshape(3, NA_QROWS * GRID_W, NA_KROWS * GRID_W))


def _na_bias_table(rpb, rows):
    row_sel, col_sel, ok = _na_bias_selectors(rows)
    hi = lax.Precision.HIGHEST
    picked_rows = jnp.einsum('gr,hrd->hgd', row_sel, rpb, precision=hi)
    band = jnp.einsum('hgd,dn->hgn', picked_rows, col_sel, precision=hi)
    band = band.reshape(-1, 3, NA_QROWS, NA_KROWS, GRID_W, GRID_W).transpose(0, 1, 2, 4, 3, 5)
    return jnp.where(ok[None], band.reshape(-1, 3, NA_QROWS * GRID_W, NA_KROWS * GRID_W), NEG_BIG)


def _neighbourhood_attention(na, bias, layer, bsz, seq):
    rows = seq // GRID_W
    nj = rows // NA_QROWS
    qn = NA_QROWS * GRID_W
    kn = NA_KROWS * GRID_W
    na3 = na.reshape(bsz, seq, 3 * NA_W)
    kstart = lambda j: jnp.clip(j - 1, 0, nj - 3)
    pattern = lambda j: jnp.minimum(j, 1) + jnp.maximum(j - (nj - 2), 0)
    kv_specs = [pl.BlockSpec((None, qn, NA_W), functools.partial(lambda j, b, i, part: (b, kstart(j) + i, part),
                                                                 i=i, part=part))
                for part in (1, 2) for i in range(3)]
    out = pl.pallas_call(
        _na_kernel,
        grid=(nj, bsz),
        in_specs=[pl.BlockSpec((None, qn, NA_W), lambda j, b: (b, j, 0))] + kv_specs + [
            pl.BlockSpec((None, NA_HEADS, None, qn, kn), lambda j, b: (layer, 0, pattern(j), 0, 0))],
        out_specs=pl.BlockSpec((None, qn, NA_W), lambda j, b: (b, j, 0)),
        out_shape=jax.ShapeDtypeStruct((bsz, seq, NA_W), F32),
        compiler_params=_cparams(2),
        name="neighbourhood_attn",
    )(na3, na3, na3, na3, na3, na3, na3, bias)
    return out.reshape(bsz * seq, NA_W)


def _mem_kv_kernel(mem_ref, g_ref, w_ref, gk_ref, k_ref, v_ref):
    h = _rms_rows(mem_ref[...], g_ref[...]).astype(BF16)
    kv = _dot(h, w_ref[...])
    k_ref[...] = _per_head_rms(kv[:, :MEM_W], gk_ref[...], MEM_DH).astype(BF16)
    v_ref[...] = kv[:, MEM_W:].astype(BF16)


def _mem_kv(mem, g_mem, w_kv, gk, layer):
    bsz, n_mem, _ = mem.shape
    return pl.pallas_call(
        _mem_kv_kernel,
        grid=(bsz,),
        in_specs=[pl.BlockSpec((None, n_mem, D_MODEL), lambda b: (b, 0, 0)),
                  pl.BlockSpec((None, 1, D_MODEL), lambda b: (layer, 0, 0)),
                  pl.BlockSpec((None, D_MODEL, 2 * MEM_W), lambda b: (layer, 0, 0)),
                  pl.BlockSpec((None, 1, MEM_W), lambda b: (layer, 0, 0))],
        out_specs=[pl.BlockSpec((None, n_mem, MEM_W), lambda b: (b, 0, 0))] * 2,
        out_shape=[jax.ShapeDtypeStruct((bsz, n_mem, MEM_W), BF16)] * 2,
        compiler_params=_cparams(1),
        name="mem_kv",
    )(mem, g_mem, w_kv, gk)


def _mem_attn_kernel(q_ref, k_ref, v_ref, o_ref):
    q = q_ref[...]
    k_b = k_ref[...]
    v_b = v_ref[...]
    zero = jnp.zeros((), q.dtype)
    acc = jnp.zeros(o_ref.shape, F32)
    for h in range(MEM_HEADS):
        hm = _head_mask(MEM_W, MEM_DH, h)
        p = _softmax_rows(_dot_nt(jnp.where(hm, q, zero), k_b)).astype(BF16)
        acc = acc + _dot(p, jnp.where(hm, v_b, zero))
    o_ref[...] = acc


def _mem_attention(cq, k_m, v_m, bsz, seq, tq):
    n_mem = k_m.shape[1]
    out = pl.pallas_call(
        _mem_attn_kernel,
        grid=(bsz, seq // tq),
        in_specs=[pl.BlockSpec((None, tq, MEM_W), lambda b, i: (b, i, 0)),
                  pl.BlockSpec((None, n_mem, MEM_W), lambda b, i: (b, 0, 0)),
                  pl.BlockSpec((None, n_mem, MEM_W), lambda b, i: (b, 0, 0))],
        out_specs=pl.BlockSpec((None, tq, MEM_W), lambda b, i: (b, i, 0)),
        out_shape=jax.ShapeDtypeStruct((bsz, seq, MEM_W), F32),
        compiler_params=_cparams(2),
        name="mem_attn",
    )(cq.reshape(bsz, seq, MEM_W), k_m, v_m)
    return out.reshape(bsz * seq, MEM_W)


def _outproj_kernel(hf_ref, hb_ref, o_ref, yn_ref, yc_ref, x_ref, w_ref, gh_ref, gf_ref, wr_ref, xz_ref, afft_ref):
    for rows in _row_groups(x_ref.shape[0], OUT_PROJ_ROW_GROUPS):
        hs = hf_ref[rows, :] + hb_ref[rows, :]
        acc = x_ref[rows, :]
        for h in range(M_HEADS):
            sl = slice(h * M_DV, (h + 1) * M_DV)
            y = jax.nn.sigmoid(o_ref[rows, sl]) * _rms_rows(hs[:, sl], gh_ref[:, sl])
            acc = acc + _dot(y.astype(BF16), w_ref[sl, :])
        acc = acc + _dot(yn_ref[rows, :].astype(BF16), w_ref[V_W:V_W + NA_W, :])
        acc = acc + _dot(yc_ref[rows, :].astype(BF16), w_ref[V_W + NA_W:, :])
        xz_ref[rows, XZ_X[0]:XZ_X[1]] = acc
        h2 = _rms_rows(acc, gf_ref[...])
        xz_ref[rows, XZ_H[0]:XZ_H[1]] = h2
        h2_hi, h2_lo = _split_bf16(h2, 2)
        part = _dot(h2_hi, wr_ref[...]) + _dot(h2_lo, wr_ref[...])
        logits = part + pltpu.roll(part, LANES - N_EXPERTS, 1) + pltpu.roll(part, LANES - 2 * N_EXPERTS, 1)
        lane = lax.broadcasted_iota(jnp.int32, logits.shape, 1)
        aff = _softmax_rows(jnp.where(lane < N_EXPERTS, logits, NEG_BIG))
        xz_ref[rows, XZ_A[0]:XZ_A[1]] = aff
        afft_ref[:, rows] = aff.T[:N_EXPERTS, :]


def _outproj(h_fw, h_bw, o, y_n, y_c, x_src, w_out, g_head, g_ffn, w_router_p, layer, tm):
    n = x_src.shape[0]
    row = lambda w_: pl.BlockSpec((tm, w_), lambda i: (i, 0))
    per_layer = lambda a: pl.BlockSpec((None,) + a.shape[1:], lambda i: (layer, 0, 0))
    return pl.pallas_call(
        _outproj_kernel,
        grid=(n // tm,),
        in_specs=[row(V_W), row(V_W), row(V_W), row(NA_W), row(MEM_W), row(D_MODEL), per_layer(w_out),
                  per_layer(g_head), per_layer(g_ffn), per_layer(w_router_p)],
        out_specs=[row(XZ_W), pl.BlockSpec((N_EXPERTS, tm), lambda i: (0, i))],
        out_shape=[jax.ShapeDtypeStruct((n, XZ_W), F32), jax.ShapeDtypeStruct((N_EXPERTS, n), F32)],
        compiler_params=_cparams(1),
        name="out_proj_router",
    )(h_fw, h_bw, o, y_n, y_c, x_src, w_out, g_head, g_ffn, w_router_p)


def _lane_cumsum(mask_f, tri_b):
    n = mask_f.shape[1]
    run = jnp.zeros((mask_f.shape[0], 1), F32)
    parts = []
    for j in range(n // LANES):
        local = _dot(mask_f[:, j * LANES:(j + 1) * LANES].astype(BF16), tri_b) + run
        parts.append(local)
        run = local[:, LANES - 1:LANES]
    return jnp.concatenate(parts, axis=1)


def _select_kernel(aff_ref, idx_ref, csum_ref, *, cap):
    bits = pltpu.bitcast(aff_ref[...], jnp.int32)
    n_e, n_tok = bits.shape
    thr = jnp.zeros((n_e, 1), jnp.int32)
    for bit in range(30, -1, -1):
        cand = thr | (1 << bit)
        cnt = jnp.sum(jnp.where(bits >= cand, 1.0, 0.0), axis=1, keepdims=True)
        thr = jnp.where(cnt >= cap, cand, thr)
    s_i = lax.broadcasted_iota(jnp.int32, (LANES, LANES), 0)
    t_i = lax.broadcasted_iota(jnp.int32, (LANES, LANES), 1)
    tri_b = jnp.where(s_i <= t_i, 1.0, 0.0).astype(BF16)
    gt = bits > thr
    eq_f = jnp.where(bits == thr, 1.0, 0.0)
    need = cap - jnp.sum(jnp.where(gt, 1.0, 0.0), axis=1, keepdims=True)
    eq_rank = _lane_cumsum(eq_f, tri_b)
    sel_f = jnp.where(gt | ((eq_f > 0.5) & (eq_rank <= need)), 1.0, 0.0)
    csum = _lane_cumsum(sel_f, tri_b)
    for e in range(n_e):
        csum_ref[e] = csum[e:e + 1, :]
    slot = lax.broadcasted_iota(jnp.int32, (cap, 1), 0).astype(F32)
    lane = lax.broadcasted_iota(jnp.int32, (cap, LANES), 1)

    def per_expert(e, cols):
        acc = jnp.zeros((cap, LANES), F32)
        for j in range(n_tok // LANES):
            acc = acc + jnp.where(csum_ref[e, :, j * LANES:(j + 1) * LANES] <= slot, 1.0, 0.0)
        return jnp.where(lane == e, jnp.sum(acc, axis=1, keepdims=True), cols)

    cols = lax.fori_loop(0, n_e, per_expert, jnp.zeros((cap, LANES), F32))
    idx_ref[...] = cols.T[:n_e, :].astype(jnp.int32)


def _select(aff_t, bsz, seq, cap):
    return pl.pallas_call(
        functools.partial(_select_kernel, cap=cap),
        grid=(bsz,),
        in_specs=[pl.BlockSpec((N_EXPERTS, seq), lambda b: (0, b))],
        out_specs=pl.BlockSpec((None, N_EXPERTS, cap), lambda b: (b, 0, 0)),
        out_shape=jax.ShapeDtypeStruct((bsz, N_EXPERTS, cap), jnp.int32),
        scratch_shapes=[pltpu.VMEM((N_EXPERTS, 1, seq), F32)],
        compiler_params=_cparams(1),
        name="expert_select",
    )(aff_t)


FF_CHUNK = 512


N_ROWBUF = 3
W_CHUNKS = 8


def _experts_kernel(idx_prev, idx_cur, idx_next, xz_hbm, w1_hbm, w3_hbm, w2_hbm, out_hbm, buf, xb_ref, w1_ref,
                    w3_ref, w2_ref, st1, st3, st2, gsem, ssem, wsem, *, cap, seq, layer, chunks_per_step):
    del xz_hbm
    e, b = pl.program_id(0), pl.program_id(1)
    n_e, n_b = pl.num_programs(0), pl.num_programs(1)
    k = e * n_b + b
    wslot = lax.rem(e, 2)
    up_rows, down_rows = D_MODEL // W_CHUNKS, D_FF // W_CHUNKS

    def weight_chunk_copies(expert, c):
        up = pl.ds(pl.multiple_of(c * up_rows, up_rows), up_rows)
        down = pl.ds(pl.multiple_of(c * down_rows, down_rows), down_rows)
        return (pltpu.make_async_copy(w1_hbm.at[layer, expert, up], st1, wsem.at[0]),
                pltpu.make_async_copy(w3_hbm.at[layer, expert, up], st3, wsem.at[1]),
                pltpu.make_async_copy(w2_hbm.at[layer, expert, down], st2, wsem.at[2]))

    def store_weight_chunk(slot, c):
        up = pl.ds(pl.multiple_of(c * up_rows, up_rows), up_rows)
        down = pl.ds(pl.multiple_of(c * down_rows, down_rows), down_rows)
        w1_ref[slot, up, :] = st1[...].astype(BF16)
        w3_ref[slot, up, :] = st3[...].astype(BF16)
        w2_ref[slot, down, :] = st2[...].astype(BF16)

    @pl.when(k == 0)
    def _():
        def load_chunk(c, carry):
            copies = weight_chunk_copies(0, c)
            for cp in copies:
                cp.start()
            for cp in copies:
                cp.wait()
            store_weight_chunk(0, c)
            return carry
        lax.fori_loop(0, W_CHUNKS, load_chunk, 0)
    slot_cur = lax.rem(k, N_ROWBUF)
    slot_next = lax.rem(k + 1, N_ROWBUF)
    slot_prev = lax.rem(k + 2, N_ROWBUF)
    base_cur = b * seq
    base_next = jnp.where(b + 1 == n_b, 0, b + 1) * seq
    base_prev = jnp.where(b == 0, n_b - 1, b - 1) * seq

    def fetch_row(idx_ref, base, slot, i):
        return pltpu.make_async_copy(out_hbm.at[pl.ds(base + idx_ref[0, i], 1)], buf.at[slot, pl.ds(i, 1)],
                                     gsem.at[slot])

    def writeback_row(idx_ref, base, slot, i):
        return pltpu.make_async_copy(buf.at[slot, pl.ds(i, 1), pl.ds(XZ_X[0], D_MODEL)],
                                     out_hbm.at[pl.ds(base + idx_ref[0, i], 1), pl.ds(XZ_X[0], D_MODEL)],
                                     ssem.at[slot])

    def wait_fetched(slot):
        pltpu.make_async_copy(out_hbm.at[pl.ds(0, cap)], buf.at[slot], gsem.at[slot]).wait()

    def wait_written(slot):
        pltpu.make_async_copy(buf.at[slot, :, pl.ds(XZ_X[0], D_MODEL)],
                              out_hbm.at[pl.ds(0, cap), pl.ds(XZ_X[0], D_MODEL)], ssem.at[slot]).wait()

    def start_all(make_copy):
        def body(i, carry):
            make_copy(i).start()
            return carry
        lax.fori_loop(0, cap, body, 0)

    @pl.when(k == 0)
    def _():
        start_all(lambda i: fetch_row(idx_prev, base_prev, slot_prev, i))
        wait_fetched(slot_prev)
        start_all(lambda i: fetch_row(idx_cur, base_cur, slot_cur, i))

    next_chunk = b * chunks_per_step
    stream_on = e + 1 < n_e

    @pl.when(stream_on & (next_chunk < W_CHUNKS))
    def _():
        for cp in weight_chunk_copies(e + 1, next_chunk):
            cp.start()

    wait_fetched(slot_cur)
    lane = lax.broadcasted_iota(jnp.int32, (cap, LANES), 1)
    gate = jnp.sum(jnp.where(lane == e, buf[slot_cur, :, XZ_A[0]:XZ_A[1]], 0.0), axis=1, keepdims=True)
    n_ff = D_FF // FF_CHUNK
    per_chunk = 2 * cap // n_ff
    xb_ref[...] = buf[slot_cur, :, XZ_H[0]:XZ_H[1]].astype(BF16)
    for f in range(n_ff):
        xb = xb_ref[...]
        for i in range(f * per_chunk, (f + 1) * per_chunk):
            if i < cap:
                fetch_row(idx_next, base_next, slot_next, i).start()
            else:
                writeback_row(idx_prev, base_prev, slot_prev, i - cap).start()
        sl = slice(f * FF_CHUNK, (f + 1) * FF_CHUNK)
        h1 = _dot(xb, w1_ref[wslot, :, sl])
        h3 = _dot(xb, w3_ref[wslot, :, sl])
        hid = (h1 * jax.nn.sigmoid(h1) * h3).astype(BF16)
        buf[slot_cur, :, XZ_X[0]:XZ_X[1]] = (buf[slot_cur, :, XZ_X[0]:XZ_X[1]]
                                             + _dot(hid, w2_ref[wslot, sl, :]) * gate)
    wait_written(slot_prev)

    for j in range(chunks_per_step):
        @pl.when(stream_on & (next_chunk + j < W_CHUNKS))
        def _(j=j):
            copies = weight_chunk_copies(e + 1, next_chunk + j)
            if j > 0:
                for cp in copies:
                    cp.start()
            for cp in copies:
                cp.wait()
            store_weight_chunk(1 - wslot, next_chunk + j)

    @pl.when(k == n_e * n_b - 1)
    def _():
        wait_fetched(slot_next)
        start_all(lambda i: writeback_row(idx_cur, base_cur, slot_cur, i))
        wait_written(slot_cur)


def _experts(idx, xz, w1, w3, w2, layer, bsz, seq, cap):
    assert bsz >= N_ROWBUF, "rows in flight for neighbouring steps must belong to different sequences"
    idx3 = idx.reshape(bsz * N_EXPERTS, 1, cap)

    def idx_spec(step):
        def index_map(e, b):
            b2 = lax.rem(b + step + bsz, bsz)
            e2 = jnp.clip(e + (b + step + bsz) // bsz - 1, 0, N_EXPERTS - 1)
            return (b2 * N_EXPERTS + e2, 0, 0)
        return pl.BlockSpec((None, 1, cap), index_map, memory_space=pltpu.SMEM)

    any_space = pl.BlockSpec(memory_space=pl.ANY)
    chunks_per_step = -(-W_CHUNKS // bsz)
    return pl.pallas_call(
        functools.partial(_experts_kernel, cap=cap, seq=seq, layer=layer, chunks_per_step=chunks_per_step),
        grid=(N_EXPERTS, bsz),
        in_specs=[idx_spec(-1), idx_spec(0), idx_spec(1), any_space, any_space, any_space, any_space],
        out_specs=any_space,
        out_shape=jax.ShapeDtypeStruct(xz.shape, F32),
        scratch_shapes=[pltpu.VMEM((N_ROWBUF, cap, XZ_W), F32), pltpu.VMEM((cap, D_MODEL), BF16),
                        pltpu.VMEM((2, D_MODEL, D_FF), BF16), pltpu.VMEM((2, D_MODEL, D_FF), BF16),
                        pltpu.VMEM((2, D_FF, D_MODEL), BF16),
                        pltpu.VMEM((D_MODEL // W_CHUNKS, D_FF), F32), pltpu.VMEM((D_MODEL // W_CHUNKS, D_FF), F32),
                        pltpu.VMEM((D_FF // W_CHUNKS, D_MODEL), F32),
                        pltpu.SemaphoreType.DMA((N_ROWBUF,)), pltpu.SemaphoreType.DMA((N_ROWBUF,)),
                        pltpu.SemaphoreType.DMA((3,))],
        input_output_aliases={3: 0},
        compiler_params=_cparams(2),
        name="experts",
    )(idx3, idx3, idx3, xz, w1, w3, w2)


def _repack_w_in(w_in):
    mlstm_w = 2 * M_HEADS * M_DQK + 2 * M_HEADS * M_DV
    pad = jnp.zeros(w_in.shape[:2] + (LANES - N_GATES,), w_in.dtype)
    return jnp.concatenate([w_in[..., :mlstm_w + N_GATES], pad, w_in[..., mlstm_w + N_GATES:]], axis=-1).astype(BF16)


def kernel(x, mem, g_mix, w_in, b_gates, conv_qk, g_mlstm_head, na_gq, na_gk, na_rpb, g_mem, w_mem_kv, mem_gq,
           mem_gk, w_out, g_ffn, w_router, w1, w3, w2):
    bsz, seq, _ = x.shape
    depth = w_in.shape[0]
    rows = seq // GRID_W
    cap = EC_CAPACITY * seq // N_EXPERTS
    tm = 512

    w_in_b = _repack_w_in(w_in)
    w_out_b = w_out.astype(BF16)
    w_kv_b = w_mem_kv.astype(BF16)
    b_gate_p = jnp.pad(b_gates, ((0, 0), (0, LANES - N_GATES)))[:, None, :]
    w_router_p = jnp.pad(jnp.concatenate(_split_bf16(w_router, 3), axis=-1),
                         ((0, 0), (0, 0), (0, LANES - 3 * N_EXPERTS)))
    na_bias = jax.vmap(lambda r: _na_bias_table(r, rows))(na_rpb)
    row3 = lambda a: a[:, None, :]
    per_head = lambda a, n_heads: row3(jnp.tile(a, (1, n_heads)))
    g_mix3, g_mem3, g_head3, g_ffn3 = row3(g_mix), row3(g_mem), row3(g_mlstm_head), row3(g_ffn)
    na_gq3, na_gk3 = per_head(na_gq, NA_HEADS), per_head(na_gk, NA_HEADS)
    mem_gq3, mem_gk3 = per_head(mem_gq, MEM_HEADS), per_head(mem_gk, MEM_HEADS)

    xs = x.reshape(bsz * seq, D_MODEL)
    for l in range(depth):
        qk, v, o, gt, na, cq = _inproj(xs, g_mix3, w_in_b, na_gq3, na_gk3, mem_gq3, l, tm)
        h_fw, h_bw = _mlstm(qk, v, gt, conv_qk, b_gate_p, l, bsz, seq)
        y_n = _neighbourhood_attention(na, na_bias, l, bsz, seq)
        k_m, v_m = _mem_kv(mem, g_mem3, w_kv_b, mem_gk3, l)
        y_c = _mem_attention(cq, k_m, v_m, bsz, seq, tm)
        xz, aff_t = _outproj(h_fw, h_bw, o, y_n, y_c, xs, w_out_b, g_head3, g_ffn3, w_router_p, l, tm)
        idx = _select(aff_t, bsz, seq, cap)
        xs = _experts(idx, xz, w1, w3, w2, l, bsz, seq, cap)
    return xs[:, XZ_X[0]:XZ_X[1]].reshape(bsz, seq, D_MODEL)
```

```python
import functools

import numpy as np
import jax
import jax.numpy as jnp
from jax import lax
from jax.experimental import pallas as pl
from jax.experimental.pallas import tpu as pltpu

F32 = jnp.float32
BF16 = jnp.bfloat16

D_MODEL = 1024
GRID_W = 64
M_HEADS = 4
M_DQK = 64
M_DV = 128
M_CHUNK = 128
M_CONV = 5
NA_HEADS = 4
NA_DH = 64
NA_KH = 8
NA_KW = 16
MEM_HEADS = 4
MEM_DH = 64
N_EXPERTS = 16
EC_CAPACITY = 2
D_FF = 2048
N_GATES = 4 * M_HEADS
EPS = 1e-6

LANES = 128
QK_W = 2 * M_HEADS * M_DQK
V_W = M_HEADS * M_DV
NA_W = NA_HEADS * NA_DH
MEM_W = MEM_HEADS * MEM_DH
SEG_QK = (0, QK_W)
SEG_V = (SEG_QK[1], SEG_QK[1] + V_W)
SEG_O = (SEG_V[1], SEG_V[1] + V_W)
SEG_G = (SEG_O[1], SEG_O[1] + LANES)
SEG_NA = (SEG_G[1], SEG_G[1] + 3 * NA_W)
SEG_CQ = (SEG_NA[1], SEG_NA[1] + MEM_W)
PROJ_W = SEG_CQ[1]
XZ_X = (0, D_MODEL)
XZ_H = (XZ_X[1], XZ_X[1] + D_MODEL)
XZ_A = (XZ_H[1], XZ_H[1] + LANES)
XZ_W = XZ_A[1]
NA_QROWS = 4
NA_KROWS = NA_QROWS + NA_KH
NEG_BIG = -1e30
VMEM_LIMIT = 56 * 1024 * 1024
IN_PROJ_ROW_GROUPS = 1
OUT_PROJ_ROW_GROUPS = 2


def _cparams(n_axes):
    return pltpu.CompilerParams(dimension_semantics=("arbitrary",) * n_axes, vmem_limit_bytes=VMEM_LIMIT)


def _dot(a, b):
    return jnp.dot(a, b, preferred_element_type=F32)


def _dot_nt(a, b):
    return lax.dot_general(a, b, (((1,), (1,)), ((), ())), preferred_element_type=F32)


def _dot_exact_rhs(a, b_bf16, terms=3):
    return sum(_dot(part, b_bf16) for part in _split_bf16(a, terms))


def _split_bf16(w, terms):
    parts, r = [], w
    for _ in range(terms):
        p = r.astype(BF16)
        parts.append(p)
        r = r - p.astype(F32)
    return parts


def _head_mask(width, head_dim, h):
    lane = lax.broadcasted_iota(jnp.int32, (1, width), 1)
    return (lane // head_dim) == h


def _same_head_matrix(width, head_dim):
    r = lax.broadcasted_iota(jnp.int32, (width, width), 0) // head_dim
    c = lax.broadcasted_iota(jnp.int32, (width, width), 1) // head_dim
    return jnp.where(r == c, 1.0, 0.0).astype(BF16)


def _per_head_rms(x, gain, head_dim):
    ss = _dot_exact_rhs(x * x, _same_head_matrix(x.shape[1], head_dim), terms=2)
    return x * lax.rsqrt(ss * (1.0 / head_dim) + EPS) * gain


def _rms_rows(x, gain):
    ms = jnp.mean(x * x, axis=-1, keepdims=True)
    return x * lax.rsqrt(ms + EPS) * gain


def _row_groups(n_rows, groups):
    step = n_rows // groups
    return [slice(r * step, (r + 1) * step) for r in range(groups)]


def _softmax_rows(s):
    m = jnp.max(s, axis=-1, keepdims=True)
    e = jnp.exp(s - m)
    return e * (1.0 / jnp.sum(e, axis=-1, keepdims=True))


def _inproj_kernel(x_ref, g_ref, w_ref, na_gq_ref, na_gk_ref, mem_gq_ref, qk_ref, v_ref, o_ref, gt_ref, na_ref,
                   cq_ref):
    for rows in _row_groups(x_ref.shape[0], IN_PROJ_ROW_GROUPS):
        h = _rms_rows(x_ref[rows, :], g_ref[...]).astype(BF16)
        for seg, out in ((SEG_QK, qk_ref), (SEG_V, v_ref), (SEG_O, o_ref), (SEG_G, gt_ref)):
            out[rows, :] = _dot(h, w_ref[:, seg[0]:seg[1]])
        a = SEG_NA[0]
        nq = _dot(h, w_ref[:, a:a + NA_W])
        na_ref[rows, :NA_W] = (_per_head_rms(nq, na_gq_ref[...], NA_DH) * (NA_DH ** -0.5)).astype(BF16)
        nk = _dot(h, w_ref[:, a + NA_W:a + 2 * NA_W])
        na_ref[rows, NA_W:2 * NA_W] = _per_head_rms(nk, na_gk_ref[...], NA_DH).astype(BF16)
        na_ref[rows, 2 * NA_W:] = _dot(h, w_ref[:, a + 2 * NA_W:a + 3 * NA_W]).astype(BF16)
        cq = _dot(h, w_ref[:, SEG_CQ[0]:SEG_CQ[1]])
        cq_ref[rows, :] = (_per_head_rms(cq, mem_gq_ref[...], MEM_DH) * (MEM_DH ** -0.5)).astype(BF16)


def _inproj(x_src, g, w, na_gq, na_gk, mem_gq, layer, tm):
    n = x_src.shape[0]
    outs = [(SEG_QK, F32), (SEG_V, F32), (SEG_O, F32), (SEG_G, F32), (SEG_NA, BF16), (SEG_CQ, BF16)]
    per_layer_row = lambda w_: pl.BlockSpec((None, 1, w_), lambda i: (layer, 0, 0))
    return pl.pallas_call(
        _inproj_kernel,
        grid=(n // tm,),
        in_specs=[pl.BlockSpec((tm, D_MODEL), lambda i: (i, 0)),
                  per_layer_row(D_MODEL),
                  pl.BlockSpec((None, D_MODEL, PROJ_W), lambda i: (layer, 0, 0)),
                  per_layer_row(NA_W), per_layer_row(NA_W), per_layer_row(MEM_W)],
        out_specs=[pl.BlockSpec((tm, s[1] - s[0]), lambda i: (i, 0)) for s, _ in outs],
        out_shape=[jax.ShapeDtypeStruct((n, s[1] - s[0]), dt) for s, dt in outs],
        compiler_params=_cparams(1),
        name="in_proj",
    )(x_src, g, w, na_gq, na_gk, mem_gq)


HALO = 8


def _log_sigmoid(x):
    return -(jnp.maximum(-x, 0.0) + jnp.log(1.0 + jnp.exp(-jnp.abs(x))))


def _qk_conv_kernel(qk_ref, prev_ref, next_ref, cw_ref, q_ref, k_ref):
    i = pl.program_id(1)
    rows = qk_ref.shape[0]
    prev_on = jnp.where(i > 0, 1.0, 0.0)
    next_on = jnp.where(i < pl.num_programs(1) - 1, 1.0, 0.0)
    ext = jnp.concatenate([prev_ref[...] * prev_on, qk_ref[...], next_ref[...] * next_on], axis=0)
    pad = M_CONV // 2
    conv = ext[HALO - pad:HALO - pad + rows] * cw_ref[0:1, :]
    for j in range(1, M_CONV):
        conv = conv + ext[HALO - pad + j:HALO - pad + j + rows] * cw_ref[j:j + 1, :]
    qk = conv * jax.nn.sigmoid(conv)
    q_ref[...] = (qk[:, :QK_W // 2] * (M_DQK ** -0.5)).astype(BF16)
    k_ref[...] = qk[:, QK_W // 2:]


def _qk_conv(qk, conv_w, layer, bsz, seq, rows):
    hb = rows // HALO
    n_halo = seq // HALO
    return pl.pallas_call(
        _qk_conv_kernel,
        grid=(bsz, seq // rows),
        in_specs=[pl.BlockSpec((None, rows, QK_W), lambda b, i: (b, i, 0)),
                  pl.BlockSpec((None, HALO, QK_W), lambda b, i: (b, jnp.maximum(i * hb - 1, 0), 0)),
                  pl.BlockSpec((None, HALO, QK_W), lambda b, i: (b, jnp.minimum((i + 1) * hb, n_halo - 1), 0)),
                  pl.BlockSpec((None, M_CONV, QK_W), lambda b, i: (layer, 0, 0))],
        out_specs=[pl.BlockSpec((None, rows, QK_W // 2), lambda b, i: (b, i, 0))] * 2,
        out_shape=[jax.ShapeDtypeStruct((bsz, seq, QK_W // 2), BF16),
                   jax.ShapeDtypeStruct((bsz, seq, QK_W // 2), F32)],
        compiler_params=_cparams(2),
        name="mlstm_qk_conv",
    )(qk, qk, qk, conv_w)


def _lane_broadcast_columns(x, first, count):
    r = lax.broadcasted_iota(jnp.int32, (LANES, count * LANES), 0)
    c = lax.broadcasted_iota(jnp.int32, (LANES, count * LANES), 1)
    pick = jnp.where(r == first + c // LANES, 1.0, 0.0).astype(BF16)
    return _dot_exact_rhs(x, pick)


def _mlstm_direction(reverse, q_ref, k_ref, v_ref, g_ref, bg_ref, h_ref, c_ref, m_ref):
    L = M_CHUNK
    gates = g_ref[...] + bg_ref[...]
    log_f = _log_sigmoid(gates)
    t_i = lax.broadcasted_iota(jnp.int32, (L, L), 0)
    s_i = lax.broadcasted_iota(jnp.int32, (L, L), 1)
    visible = (s_i >= t_i) if reverse else (s_i <= t_i)
    b_all = _masked_prefix_sum(visible, log_f)
    off = 2 * M_HEADS if reverse else 0
    last = 0 if reverse else L - 1
    b_cols = _lane_broadcast_columns(b_all, off + M_HEADS, M_HEADS)
    li_cols = _lane_broadcast_columns(gates, off, M_HEADS)
    row_term_t = (gates - pltpu.roll(b_all, LANES - M_HEADS, 1)).T
    ones_b = jnp.ones((L, M_DV), BF16)
    heads_per_block = LANES // M_DQK

    for h in range(M_HEADS):
        blk = slice((h // heads_per_block) * LANES, (h // heads_per_block + 1) * LANES)
        hm = _head_mask(LANES, M_DQK, h % heads_per_block)
        hs = slice(h * LANES, (h + 1) * LANES)
        sidx = (M_HEADS if reverse else 0) + h
        b_col = b_cols[:, hs]
        m_prev = m_ref[sidx:sidx + 1, :]
        g_tot = b_col[last:last + 1, :]
        c_prev_b = c_ref[sidx].astype(BF16)
        q_hb = jnp.where(hm, q_ref[:, blk], jnp.zeros((), BF16))
        k_blk = k_ref[:, blk]
        v_hb = v_ref[:, h * M_DV:(h + 1) * M_DV].astype(BF16)

        d_log = jnp.where(visible, b_col + row_term_t[off + h:off + h + 1, :], -jnp.inf)
        inter = b_col + m_prev
        m_t = jnp.maximum(inter, jnp.max(d_log, axis=-1, keepdims=True))
        s_inter = jnp.exp(inter - m_t)
        scores = _dot_nt(q_hb, k_blk.astype(BF16)) * jnp.exp(d_log - m_t)
        qcn = _dot(q_hb, c_prev_b)
        num = s_inter * qcn[:, :M_DV] + _dot(scores.astype(BF16), v_hb)
        den = s_inter * qcn[:, M_DV:] + jnp.sum(scores, axis=-1, keepdims=True)
        h_ref[:, h * M_DV:(h + 1) * M_DV] = num / jnp.maximum(jnp.abs(den), jnp.exp(-m_t))

        w_log = g_tot - b_col + li_cols[:, hs]
        a = jnp.max(w_log, axis=0, keepdims=True)
        kw_t = (k_blk * jnp.exp(w_log - a)).T.astype(BF16)
        kvn = _dot(kw_t, jnp.concatenate([v_hb, ones_b], axis=1))
        m_new = jnp.maximum(g_tot + m_prev, a)
        s_prev = jnp.exp(g_tot + m_prev - m_new)
        s_cur = jnp.exp(a - m_new)
        two = lambda r: jnp.concatenate([r, r], axis=1)
        c_ref[sidx] = two(s_prev) * c_ref[sidx] + two(s_cur) * kvn
        m_ref[sidx:sidx + 1, :] = m_new


def _masked_prefix_sum(visible, x):
    tri = jnp.where(visible, 1.0, 0.0).astype(BF16)
    x0, x1, x2 = _split_bf16(x, 3)
    return _dot(tri, x0) + _dot(tri, x1) + _dot(tri, x2)


def _mlstm_kernel(qf, kf, vf, gf, qb, kb, vb, gb, bg_ref, hf_ref, hb_ref, c_ref, m_ref):
    @pl.when(pl.program_id(1) == 0)
    def _():
        c_ref[...] = jnp.zeros_like(c_ref)
        m_ref[...] = jnp.zeros_like(m_ref)

    _mlstm_direction(False, qf, kf, vf, gf, bg_ref, hf_ref, c_ref, m_ref)
    _mlstm_direction(True, qb, kb, vb, gb, bg_ref, hb_ref, c_ref, m_ref)


def _mlstm(qk, v, gt, conv_w, b_gate, layer, bsz, seq):
    L = M_CHUNK
    nc = seq // L
    q_b, k = _qk_conv(qk.reshape(bsz, seq, QK_W), conv_w, layer, bsz, seq, 4 * L)
    v3 = v.reshape(bsz, seq, V_W)
    g3 = gt.reshape(bsz, seq, LANES)

    def specs(cidx):
        chunk = lambda w_: pl.BlockSpec((None, L, w_), lambda b, c: (b, cidx(c), 0))
        return [chunk(QK_W // 2), chunk(QK_W // 2), chunk(V_W), chunk(LANES)]

    fw = lambda c: c
    bw = lambda c: nc - 1 - c
    h_fw, h_bw = pl.pallas_call(
        _mlstm_kernel,
        grid=(bsz, nc),
        in_specs=specs(fw) + specs(bw) + [pl.BlockSpec((None, 1, LANES), lambda b, c: (layer, 0, 0))],
        out_specs=[pl.BlockSpec((None, L, V_W), lambda b, c: (b, c, 0)),
                   pl.BlockSpec((None, L, V_W), lambda b, c: (b, nc - 1 - c, 0))],
        out_shape=[jax.ShapeDtypeStruct((bsz, seq, V_W), F32)] * 2,
        scratch_shapes=[pltpu.VMEM((2 * M_HEADS, LANES, 2 * M_DV), F32),
                        pltpu.VMEM((2 * M_HEADS, LANES), F32)],
        compiler_params=_cparams(2),
        name="mlstm",
    )(q_b, k, v3, g3, q_b, k, v3, g3, b_gate)
    return h_fw.reshape(bsz * seq, V_W), h_bw.reshape(bsz * seq, V_W)


def _na_kernel(q_ref, k0, k1, k2, v0, v1, v2, bias_ref, o_ref):
    q = q_ref[...]
    k = jnp.concatenate([k0[...], k1[...], k2[...]], axis=0)
    v = jnp.concatenate([v0[...], v1[...], v2[...]], axis=0)
    zero = jnp.zeros((), q.dtype)
    acc = jnp.zeros(o_ref.shape, F32)
    for h in range(NA_HEADS):
        hm = _head_mask(NA_W, NA_DH, h)
        s = _dot_nt(jnp.where(hm, q, zero), k) + bias_ref[h]
        p = _softmax_rows(s).astype(BF16)
        acc = acc + _dot(p, jnp.where(hm, v, zero))
    o_ref[...] = acc


def _na_bias_selectors(rows):
    nj = rows // NA_QROWS
    qi = np.arange(NA_QROWS)
    ki = np.arange(NA_KROWS)
    row_sel = np.zeros((3, NA_QROWS, NA_KROWS, 2 * NA_KH - 1), np.float32)
    row_ok = np.zeros((3, NA_QROWS, NA_KROWS), bool)
    for pat, j in enumerate((0, 1, nj - 1)):
        ks = min(max(NA_QROWS * j - NA_KH // 2, 0), rows - NA_KROWS)
        r = NA_QROWS * j + qi
        rs = np.clip(r - NA_KH // 2, 0, rows - NA_KH)
        krow = ks + ki
        ok = (krow[None, :] >= rs[:, None]) & (krow[None, :] < rs[:, None] + NA_KH)
        dr = krow[None, :] - r[:, None] + NA_KH - 1
        for a in range(NA_QROWS):
            for c in range(NA_KROWS):
                if ok[a, c]:
                    row_sel[pat, a, c, dr[a, c]] = 1.0
        row_ok[pat] = ok
    qc = np.arange(GRID_W)
    cs = np.clip(qc - NA_KW // 2, 0, GRID_W - NA_KW)
    col_ok = (qc[None, :] >= cs[:, None]) & (qc[None, :] < cs[:, None] + NA_KW)
    dc = qc[None, :] - qc[:, None] + NA_KW - 1
    col_sel = np.zeros((2 * NA_KW - 1, GRID_W, GRID_W), np.float32)
    for a in range(GRID_W):
        for c in range(GRID_W):
            if col_ok[a, c]:
                col_sel[dc[a, c], a, c] = 1.0
    ok = row_ok[:, :, None, :, None] & col_ok[None, None, :, None, :]
    return (row_sel.reshape(3 * NA_QROWS * NA_KROWS, -1), col_sel.reshape(2 * NA_KW - 1, -1),
            ok.reshape(3, NA_QROWS * GRID_W, NA_KROWS * GRID_W))


def _na_bias_table(rpb, rows):
    row_sel, col_sel, ok = _na_bias_selectors(rows)
    hi = lax.Precision.HIGHEST
    picked_rows = jnp.einsum('gr,hrd->hgd', row_sel, rpb, precision=hi)
    band = jnp.einsum('hgd,dn->hgn', picked_rows, col_sel, precision=hi)
    band = band.reshape(-1, 3, NA_QROWS, NA_KROWS, GRID_W, GRID_W).transpose(0, 1, 2, 4, 3, 5)
    return jnp.where(ok[None], band.reshape(-1, 3, NA_QROWS * GRID_W, NA_KROWS * GRID_W), NEG_BIG)


def _neighbourhood_attention(na, bias, layer, bsz, seq):
    rows = seq // GRID_W
    nj = rows // NA_QROWS
    qn = NA_QROWS * GRID_W
    kn = NA_KROWS * GRID_W
    na3 = na.reshape(bsz, seq, 3 * NA_W)
    kstart = lambda j: jnp.clip(j - 1, 0, nj - 3)
    pattern = lambda j: jnp.minimum(j, 1) + jnp.maximum(j - (nj - 2), 0)
    kv_specs = [pl.BlockSpec((None, qn, NA_W), functools.partial(lambda j, b, i, part: (b, kstart(j) + i, part),
                                                                 i=i, part=part))
                for part in (1, 2) for i in range(3)]
    out = pl.pallas_call(
        _na_kernel,
        grid=(nj, bsz),
        in_specs=[pl.BlockSpec((None, qn, NA_W), lambda j, b: (b, j, 0))] + kv_specs + [
            pl.BlockSpec((None, NA_HEADS, None, qn, kn), lambda j, b: (layer, 0, pattern(j), 0, 0))],
        out_specs=pl.BlockSpec((None, qn, NA_W), lambda j, b: (b, j, 0)),
        out_shape=jax.ShapeDtypeStruct((bsz, seq, NA_W), F32),
        compiler_params=_cparams(2),
        name="neighbourhood_attn",
    )(na3, na3, na3, na3, na3, na3, na3, bias)
    return out.reshape(bsz * seq, NA_W)


def _mem_kv_kernel(mem_ref, g_ref, w_ref, gk_ref, k_ref, v_ref):
    h = _rms_rows(mem_ref[...], g_ref[...]).astype(BF16)
    kv = _dot(h, w_ref[...])
    k_ref[...] = _per_head_rms(kv[:, :MEM_W], gk_ref[...], MEM_DH).astype(BF16)
    v_ref[...] = kv[:, MEM_W:].astype(BF16)


def _mem_kv(mem, g_mem, w_kv, gk, layer):
    bsz, n_mem, _ = mem.shape
    return pl.pallas_call(
        _mem_kv_kernel,
        grid=(bsz,),
        in_specs=[pl.BlockSpec((None, n_mem, D_MODEL), lambda b: (b, 0, 0)),
                  pl.BlockSpec((None, 1, D_MODEL), lambda b: (layer, 0, 0)),
                  pl.BlockSpec((None, D_MODEL, 2 * MEM_W), lambda b: (layer, 0, 0)),
                  pl.BlockSpec((None, 1, MEM_W), lambda b: (layer, 0, 0))],
        out_specs=[pl.BlockSpec((None, n_mem, MEM_W), lambda b: (b, 0, 0))] * 2,
        out_shape=[jax.ShapeDtypeStruct((bsz, n_mem, MEM_W), BF16)] * 2,
        compiler_params=_cparams(1),
        name="mem_kv",
    )(mem, g_mem, w_kv, gk)


def _mem_attn_kernel(q_ref, k_ref, v_ref, o_ref):
    q = q_ref[...]
    k_b = k_ref[...]
    v_b = v_ref[...]
    zero = jnp.zeros((), q.dtype)
    acc = jnp.zeros(o_ref.shape, F32)
    for h in range(MEM_HEADS):
        hm = _head_mask(MEM_W, MEM_DH, h)
        p = _softmax_rows(_dot_nt(jnp.where(hm, q, zero), k_b)).astype(BF16)
        acc = acc + _dot(p, jnp.where(hm, v_b, zero))
    o_ref[...] = acc


def _mem_attention(cq, k_m, v_m, bsz, seq, tq):
    n_mem = k_m.shape[1]
    out = pl.pallas_call(
        _mem_attn_kernel,
        grid=(bsz, seq // tq),
        in_specs=[pl.BlockSpec((None, tq, MEM_W), lambda b, i: (b, i, 0)),
                  pl.BlockSpec((None, n_mem, MEM_W), lambda b, i: (b, 0, 0)),
                  pl.BlockSpec((None, n_mem, MEM_W), lambda b, i: (b, 0, 0))],
        out_specs=pl.BlockSpec((None, tq, MEM_W), lambda b, i: (b, i, 0)),
        out_shape=jax.ShapeDtypeStruct((bsz, seq, MEM_W), F32),
        compiler_params=_cparams(2),
        name="mem_attn",
    )(cq.reshape(bsz, seq, MEM_W), k_m, v_m)
    return out.reshape(bsz * seq, MEM_W)


def _outproj_kernel(hf_ref, hb_ref, o_ref, yn_ref, yc_ref, x_ref, w_ref, gh_ref, gf_ref, wr_ref, xz_ref, afft_ref):
    for rows in _row_groups(x_ref.shape[0], OUT_PROJ_ROW_GROUPS):
        hs = hf_ref[rows, :] + hb_ref[rows, :]
        acc = x_ref[rows, :]
        for h in range(M_HEADS):
            sl = slice(h * M_DV, (h + 1) * M_DV)
            y = jax.nn.sigmoid(o_ref[rows, sl]) * _rms_rows(hs[:, sl], gh_ref[:, sl])
            acc = acc + _dot(y.astype(BF16), w_ref[sl, :])
        acc = acc + _dot(yn_ref[rows, :].astype(BF16), w_ref[V_W:V_W + NA_W, :])
        acc = acc + _dot(yc_ref[rows, :].astype(BF16), w_ref[V_W + NA_W:, :])
        xz_ref[rows, XZ_X[0]:XZ_X[1]] = acc
        h2 = _rms_rows(acc, gf_ref[...])
        xz_ref[rows, XZ_H[0]:XZ_H[1]] = h2
        h2_hi, h2_lo = _split_bf16(h2, 2)
        part = _dot(h2_hi, wr_ref[...]) + _dot(h2_lo, wr_ref[...])
        logits = part + pltpu.roll(part, LANES - N_EXPERTS, 1) + pltpu.roll(part, LANES - 2 * N_EXPERTS, 1)
        lane = lax.broadcasted_iota(jnp.int32, logits.shape, 1)
        aff = _softmax_rows(jnp.where(lane < N_EXPERTS, logits, NEG_BIG))
        xz_ref[rows, XZ_A[0]:XZ_A[1]] = aff
        afft_ref[:, rows] = aff.T[:N_EXPERTS, :]


def _outproj(h_fw, h_bw, o, y_n, y_c, x_src, w_out, g_head, g_ffn, w_router_p, layer, tm):
    n = x_src.shape[0]
    row = lambda w_: pl.BlockSpec((tm, w_), lambda i: (i, 0))
    per_layer = lambda a: pl.BlockSpec((None,) + a.shape[1:], lambda i: (layer, 0, 0))
    return pl.pallas_call(
        _outproj_kernel,
        grid=(n // tm,),
        in_specs=[row(V_W), row(V_W), row(V_W), row(NA_W), row(MEM_W), row(D_MODEL), per_layer(w_out),
                  per_layer(g_head), per_layer(g_ffn), per_layer(w_router_p)],
        out_specs=[row(XZ_W), pl.BlockSpec((N_EXPERTS, tm), lambda i: (0, i))],
        out_shape=[jax.ShapeDtypeStruct((n, XZ_W), F32), jax.ShapeDtypeStruct((N_EXPERTS, n), F32)],
        compiler_params=_cparams(1),
        name="out_proj_router",
    )(h_fw, h_bw, o, y_n, y_c, x_src, w_out, g_head, g_ffn, w_router_p)


def _lane_cumsum(mask_f, tri_b):
    n = mask_f.shape[1]
    run = jnp.zeros((mask_f.shape[0], 1), F32)
    parts = []
    for j in range(n // LANES):
        local = _dot(mask_f[:, j * LANES:(j + 1) * LANES].astype(BF16), tri_b) + run
        parts.append(local)
        run = local[:, LANES - 1:LANES]
    return jnp.concatenate(parts, axis=1)


def _select_kernel(aff_ref, idx_ref, csum_ref, cend_ref, *, cap):
    bits = pltpu.bitcast(aff_ref[...], jnp.int32)
    n_e, n_tok = bits.shape
    thr = jnp.zeros((n_e, 1), jnp.int32)
    for bit in range(30, -1, -1):
        cand = thr | (1 << bit)
        cnt = jnp.sum(jnp.where(bits >= cand, 1.0, 0.0), axis=1, keepdims=True)
        thr = jnp.where(cnt >= cap, cand, thr)
    s_i = lax.broadcasted_iota(jnp.int32, (LANES, LANES), 0)
    t_i = lax.broadcasted_iota(jnp.int32, (LANES, LANES), 1)
    tri_b = jnp.where(s_i <= t_i, 1.0, 0.0).astype(BF16)
    gt = bits > thr
    eq_f = jnp.where(bits == thr, 1.0, 0.0)
    need = cap - jnp.sum(jnp.where(gt, 1.0, 0.0), axis=1, keepdims=True)
    eq_rank = _lane_cumsum(eq_f, tri_b)
    sel_f = jnp.where(gt | ((eq_f > 0.5) & (eq_rank <= need)), 1.0, 0.0)
    n_chunks = n_tok // LANES
    assert n_chunks <= LANES and cap <= 16 * 256
    csum_ref[...] = jnp.zeros_like(csum_ref)
    run = jnp.zeros((n_e, 1), F32)
    chunk_lane = lax.broadcasted_iota(jnp.int32, (n_e, LANES), 1)
    cend = jnp.full((n_e, LANES), float(2 * cap), F32)
    for j in range(n_chunks):
        local = _dot(sel_f[:, j * LANES:(j + 1) * LANES].astype(BF16), tri_b) + run
        run = local[:, LANES - 1:LANES]
        cend = jnp.where(chunk_lane == j, run, cend)
        for e in range(n_e):
            csum_ref[e, j:j + 1, :] = local[e:e + 1, :]
    for e in range(n_e):
        cend_ref[e] = cend[e:e + 1, :]
    slot = lax.broadcasted_iota(jnp.int32, (cap, 1), 0).astype(F32)
    lane = lax.broadcasted_iota(jnp.int32, (cap, LANES), 1)

    def per_expert(e, cols):
        counts = csum_ref[e]
        hi = jnp.floor(counts * (1.0 / 16.0))
        lo = counts - 16.0 * hi
        full_chunks = jnp.sum(jnp.where(cend_ref[e] <= slot, 1.0, 0.0), axis=1, keepdims=True)
        pick = jnp.where(lane == full_chunks.astype(jnp.int32), 1.0, 0.0).astype(BF16)
        chunk_counts = 16.0 * _dot(pick, hi.astype(BF16)) + _dot(pick, lo.astype(BF16))
        inside = jnp.sum(jnp.where(chunk_counts <= slot, 1.0, 0.0), axis=1, keepdims=True)
        return jnp.where(lane == e, float(LANES) * full_chunks + inside, cols)

    cols = lax.fori_loop(0, n_e, per_expert, jnp.zeros((cap, LANES), F32))
    idx_ref[...] = cols.T[:n_e, :].astype(jnp.int32) + pl.program_id(0) * n_tok


def _select(aff_t, bsz, seq, cap):
    return pl.pallas_call(
        functools.partial(_select_kernel, cap=cap),
        grid=(bsz,),
        in_specs=[pl.BlockSpec((N_EXPERTS, seq), lambda b: (0, b))],
        out_specs=pl.BlockSpec((None, N_EXPERTS, cap), lambda b: (b, 0, 0)),
        out_shape=jax.ShapeDtypeStruct((bsz, N_EXPERTS, cap), jnp.int32),
        scratch_shapes=[pltpu.VMEM((N_EXPERTS, LANES, LANES), F32), pltpu.VMEM((N_EXPERTS, 1, LANES), F32)],
        compiler_params=_cparams(1),
        name="expert_select",
    )(aff_t)


FF_CHUNK = 512


N_ROWBUF = 3
W_CHUNKS = 8


def _experts_kernel(idx_prev, idx_cur, idx_next, xz_hbm, w1_hbm, w3_hbm, w2_hbm, out_hbm, buf, xb_ref, w1_ref,
                    w3_ref, w2_ref, st1, st3, st2, gsem, ssem, wsem, *, cap, layer, chunks_per_step):
    del xz_hbm
    e, b = pl.program_id(0), pl.program_id(1)
    n_e, n_b = pl.num_programs(0), pl.num_programs(1)
    k = e * n_b + b
    wslot = lax.rem(e, 2)
    up_rows, down_rows = D_MODEL // W_CHUNKS, D_FF // W_CHUNKS

    def weight_chunk_copies(expert, c):
        up = pl.ds(pl.multiple_of(c * up_rows, up_rows), up_rows)
        down = pl.ds(pl.multiple_of(c * down_rows, down_rows), down_rows)
        return (pltpu.make_async_copy(w1_hbm.at[layer, expert, up], st1, wsem.at[0]),
                pltpu.make_async_copy(w3_hbm.at[layer, expert, up], st3, wsem.at[1]),
                pltpu.make_async_copy(w2_hbm.at[layer, expert, down], st2, wsem.at[2]))

    def store_weight_chunk(slot, c):
        up = pl.ds(pl.multiple_of(c * up_rows, up_rows), up_rows)
        down = pl.ds(pl.multiple_of(c * down_rows, down_rows), down_rows)
        w1_ref[slot, up, :] = st1[...].astype(BF16)
        w3_ref[slot, up, :] = st3[...].astype(BF16)
        w2_ref[slot, down, :] = st2[...].astype(BF16)

    @pl.when(k == 0)
    def _():
        def load_chunk(c, carry):
            copies = weight_chunk_copies(0, c)
            for cp in copies:
                cp.start()
            for cp in copies:
                cp.wait()
            store_weight_chunk(0, c)
            return carry
        lax.fori_loop(0, W_CHUNKS, load_chunk, 0)
    slot_cur = lax.rem(k, N_ROWBUF)
    slot_next = lax.rem(k + 1, N_ROWBUF)
    slot_prev = lax.rem(k + 2, N_ROWBUF)

    def fetch_row(idx_ref, slot, i):
        return pltpu.make_async_copy(out_hbm.at[pl.ds(idx_ref[0, i], 1)], buf.at[slot, pl.ds(i, 1)], gsem.at[slot])

    def writeback_row(idx_ref, slot, i):
        return pltpu.make_async_copy(buf.at[slot, pl.ds(i, 1), pl.ds(XZ_X[0], D_MODEL)],
                                     out_hbm.at[pl.ds(idx_ref[0, i], 1), pl.ds(XZ_X[0], D_MODEL)], ssem.at[slot])

    def wait_fetched(slot):
        pltpu.make_async_copy(out_hbm.at[pl.ds(0, cap)], buf.at[slot], gsem.at[slot]).wait()

    def wait_written(slot):
        pltpu.make_async_copy(buf.at[slot, :, pl.ds(XZ_X[0], D_MODEL)],
                              out_hbm.at[pl.ds(0, cap), pl.ds(XZ_X[0], D_MODEL)], ssem.at[slot]).wait()

    def start_all(make_copy):
        def body(i, carry):
            make_copy(i).start()
            return carry
        lax.fori_loop(0, cap, body, 0)

    @pl.when(k == 0)
    def _():
        start_all(lambda i: fetch_row(idx_prev, slot_prev, i))
        wait_fetched(slot_prev)
        start_all(lambda i: fetch_row(idx_cur, slot_cur, i))

    @pl.when(k > 0)
    def _():
        wait_written(slot_next)

    next_chunk = b * chunks_per_step
    stream_on = e + 1 < n_e

    @pl.when(stream_on & (next_chunk < W_CHUNKS))
    def _():
        for cp in weight_chunk_copies(e + 1, next_chunk):
            cp.start()

    wait_fetched(slot_cur)
    lane = lax.broadcasted_iota(jnp.int32, (cap, LANES), 1)
    gate = jnp.sum(jnp.where(lane == e, buf[slot_cur, :, XZ_A[0]:XZ_A[1]], 0.0), axis=1, keepdims=True)
    n_ff = D_FF // FF_CHUNK
    per_chunk = 2 * cap // n_ff
    xb_ref[...] = buf[slot_cur, :, XZ_H[0]:XZ_H[1]].astype(BF16)
    for f in range(n_ff):
        xb = xb_ref[...]
        for i in range(f * per_chunk, (f + 1) * per_chunk):
            if i < cap:
                fetch_row(idx_next, slot_next, i).start()
            else:
                writeback_row(idx_prev, slot_prev, i - cap).start()
        sl = slice(f * FF_CHUNK, (f + 1) * FF_CHUNK)
        h1 = _dot(xb, w1_ref[wslot, :, sl])
        h3 = _dot(xb, w3_ref[wslot, :, sl])
        hid = (h1 * jax.nn.sigmoid(h1) * h3).astype(BF16)
        buf[slot_cur, :, XZ_X[0]:XZ_X[1]] = (buf[slot_cur, :, XZ_X[0]:XZ_X[1]]
                                             + _dot(hid, w2_ref[wslot, sl, :]) * gate)

    for j in range(chunks_per_step):
        @pl.when(stream_on & (next_chunk + j < W_CHUNKS))
        def _(j=j):
            copies = weight_chunk_copies(e + 1, next_chunk + j)
            if j > 0:
                for cp in copies:
                    cp.start()
            for cp in copies:
                cp.wait()
            store_weight_chunk(1 - wslot, next_chunk + j)

    @pl.when(k == n_e * n_b - 1)
    def _():
        wait_fetched(slot_next)
        wait_written(slot_prev)
        start_all(lambda i: writeback_row(idx_cur, slot_cur, i))
        wait_written(slot_cur)


def _experts(idx, xz, w1, w3, w2, layer, bsz, cap):
    assert bsz >= N_ROWBUF, "rows in flight for neighbouring steps must belong to different sequences"
    idx3 = idx.reshape(bsz * N_EXPERTS, 1, cap)

    def idx_spec(step):
        def index_map(e, b):
            b2 = lax.rem(b + step + bsz, bsz)
            e2 = jnp.clip(e + (b + step + bsz) // bsz - 1, 0, N_EXPERTS - 1)
            return (b2 * N_EXPERTS + e2, 0, 0)
        return pl.BlockSpec((None, 1, cap), index_map, memory_space=pltpu.SMEM)

    any_space = pl.BlockSpec(memory_space=pl.ANY)
    chunks_per_step = -(-W_CHUNKS // bsz)
    return pl.pallas_call(
        functools.partial(_experts_kernel, cap=cap, layer=layer, chunks_per_step=chunks_per_step),
        grid=(N_EXPERTS, bsz),
        in_specs=[idx_spec(-1), idx_spec(0), idx_spec(1), any_space, any_space, any_space, any_space],
        out_specs=any_space,
        out_shape=jax.ShapeDtypeStruct(xz.shape, F32),
        scratch_shapes=[pltpu.VMEM((N_ROWBUF, cap, XZ_W), F32), pltpu.VMEM((cap, D_MODEL), BF16),
                        pltpu.VMEM((2, D_MODEL, D_FF), BF16), pltpu.VMEM((2, D_MODEL, D_FF), BF16),
                        pltpu.VMEM((2, D_FF, D_MODEL), BF16),
                        pltpu.VMEM((D_MODEL // W_CHUNKS, D_FF), F32), pltpu.VMEM((D_MODEL // W_CHUNKS, D_FF), F32),
                        pltpu.VMEM((D_FF // W_CHUNKS, D_MODEL), F32),
                        pltpu.SemaphoreType.DMA((N_ROWBUF,)), pltpu.SemaphoreType.DMA((N_ROWBUF,)),
                        pltpu.SemaphoreType.DMA((3,))],
        input_output_aliases={3: 0},
        compiler_params=_cparams(2),
        name="experts",
    )(idx3, idx3, idx3, xz, w1, w3, w2)


def _repack_w_in(w_in):
    mlstm_w = 2 * M_HEADS * M_DQK + 2 * M_HEADS * M_DV
    pad = jnp.zeros(w_in.shape[:2] + (LANES - N_GATES,), w_in.dtype)
    return jnp.concatenate([w_in[..., :mlstm_w + N_GATES], pad, w_in[..., mlstm_w + N_GATES:]], axis=-1).astype(BF16)


def kernel(x, mem, g_mix, w_in, b_gates, conv_qk, g_mlstm_head, na_gq, na_gk, na_rpb, g_mem, w_mem_kv, mem_gq,
           mem_gk, w_out, g_ffn, w_router, w1, w3, w2):
    bsz, seq, _ = x.shape
    depth = w_in.shape[0]
    rows = seq // GRID_W
    cap = EC_CAPACITY * seq // N_EXPERTS
    tm = 512

    w_in_b = _repack_w_in(w_in)
    w_out_b = w_out.astype(BF16)
    w_kv_b = w_mem_kv.astype(BF16)
    b_gate_p = jnp.pad(b_gates, ((0, 0), (0, LANES - N_GATES)))[:, None, :]
    w_router_p = jnp.pad(jnp.concatenate(_split_bf16(w_router, 3), axis=-1),
                         ((0, 0), (0, 0), (0, LANES - 3 * N_EXPERTS)))
    na_bias = jax.vmap(lambda r: _na_bias_table(r, rows))(na_rpb)
    row3 = lambda a: a[:, None, :]
    per_head = lambda a, n_heads: row3(jnp.tile(a, (1, n_heads)))
    g_mix3, g_mem3, g_head3, g_ffn3 = row3(g_mix), row3(g_mem), row3(g_mlstm_head), row3(g_ffn)
    na_gq3, na_gk3 = per_head(na_gq, NA_HEADS), per_head(na_gk, NA_HEADS)
    mem_gq3, mem_gk3 = per_head(mem_gq, MEM_HEADS), per_head(mem_gk, MEM_HEADS)

    xs = x.reshape(bsz * seq, D_MODEL)
    for l in range(depth):
        qk, v, o, gt, na, cq = _inproj(xs, g_mix3, w_in_b, na_gq3, na_gk3, mem_gq3, l, tm)
        h_fw, h_bw = _mlstm(qk, v, gt, conv_qk, b_gate_p, l, bsz, seq)
        y_n = _neighbourhood_attention(na, na_bias, l, bsz, seq)
        k_m, v_m = _mem_kv(mem, g_mem3, w_kv_b, mem_gk3, l)
        y_c = _mem_attention(cq, k_m, v_m, bsz, seq, tm)
        xz, aff_t = _outproj(h_fw, h_bw, o, y_n, y_c, xs, w_out_b, g_head3, g_ffn3, w_router_p, l, tm)
        idx = _select(aff_t, bsz, seq, cap)
        xs = _experts(idx, xz, w1, w3, w2, l, bsz, cap)
    return xs[:, XZ_X[0]:XZ_X[1]].reshape(bsz, seq, D_MODEL)
```

```python
import functools

import numpy as np
import jax
import jax.numpy as jnp
from jax import lax
from jax.experimental import pallas as pl
from jax.experimental.pallas import tpu as pltpu

F32 = jnp.float32
BF16 = jnp.bfloat16

D_MODEL = 1024
GRID_W = 64
M_HEADS = 4
M_DQK = 64
M_DV = 128
M_CHUNK = 128
M_CONV = 5
NA_HEADS = 4
NA_DH = 64
NA_KH = 8
NA_KW = 16
MEM_HEADS = 4
MEM_DH = 64
N_EXPERTS = 16
EC_CAPACITY = 2
D_FF = 2048
N_GATES = 4 * M_HEADS
EPS = 1e-6

LANES = 128
QK_W = 2 * M_HEADS * M_DQK
V_W = M_HEADS * M_DV
NA_W = NA_HEADS * NA_DH
MEM_W = MEM_HEADS * MEM_DH
SEG_QK = (0, QK_W)
SEG_V = (SEG_QK[1], SEG_QK[1] + V_W)
SEG_O = (SEG_V[1], SEG_V[1] + V_W)
SEG_G = (SEG_O[1], SEG_O[1] + LANES)
SEG_NA = (SEG_G[1], SEG_G[1] + 3 * NA_W)
SEG_CQ = (SEG_NA[1], SEG_NA[1] + MEM_W)
PROJ_W = SEG_CQ[1]
XZ_X = (0, D_MODEL)
XZ_H = (XZ_X[1], XZ_X[1] + D_MODEL)
XZ_A = (XZ_H[1], XZ_H[1] + LANES)
XZ_W = XZ_A[1]
NA_QROWS = 4
NA_KROWS = NA_QROWS + NA_KH
NEG_BIG = -1e30
VMEM_LIMIT = 56 * 1024 * 1024
IN_PROJ_ROW_GROUPS = 2
OUT_PROJ_ROW_GROUPS = 4


def _cparams(n_axes):
    return pltpu.CompilerParams(dimension_semantics=("arbitrary",) * n_axes, vmem_limit_bytes=VMEM_LIMIT)


def _dot(a, b):
    return jnp.dot(a, b, preferred_element_type=F32)


def _dot_nt(a, b):
    return lax.dot_general(a, b, (((1,), (1,)), ((), ())), preferred_element_type=F32)


def _dot_exact_rhs(a, b_bf16, terms=3):
    return sum(_dot(part, b_bf16) for part in _split_bf16(a, terms))


def _split_bf16(w, terms):
    parts, r = [], w
    for _ in range(terms):
        p = r.astype(BF16)
        parts.append(p)
        r = r - p.astype(F32)
    return parts


def _head_mask(width, head_dim, h):
    lane = lax.broadcasted_iota(jnp.int32, (1, width), 1)
    return (lane // head_dim) == h


def _same_head_matrix(width, head_dim):
    r = lax.broadcasted_iota(jnp.int32, (width, width), 0) // head_dim
    c = lax.broadcasted_iota(jnp.int32, (width, width), 1) // head_dim
    return jnp.where(r == c, 1.0, 0.0).astype(BF16)


def _per_head_rms(x, gain, head_dim):
    ss = _dot_exact_rhs(x * x, _same_head_matrix(x.shape[1], head_dim), terms=2)
    return x * lax.rsqrt(ss * (1.0 / head_dim) + EPS) * gain


def _rms_rows(x, gain):
    ms = jnp.mean(x * x, axis=-1, keepdims=True)
    return x * lax.rsqrt(ms + EPS) * gain


def _row_groups(n_rows, groups):
    step = n_rows // groups
    return [slice(r * step, (r + 1) * step) for r in range(groups)]


def _softmax_rows(s):
    m = jnp.max(s, axis=-1, keepdims=True)
    e = jnp.exp(s - m)
    return e * (1.0 / jnp.sum(e, axis=-1, keepdims=True))


def _inproj_kernel(x_ref, g_ref, w_ref, na_gq_ref, na_gk_ref, mem_gq_ref, qk_ref, v_ref, o_ref, gt_ref, na_ref,
                   cq_ref):
    for rows in _row_groups(x_ref.shape[0], IN_PROJ_ROW_GROUPS):
        h = _rms_rows(x_ref[rows, :], g_ref[...]).astype(BF16)
        for seg, out in ((SEG_QK, qk_ref), (SEG_V, v_ref), (SEG_O, o_ref), (SEG_G, gt_ref)):
            out[rows, :] = _dot(h, w_ref[:, seg[0]:seg[1]])
        a = SEG_NA[0]
        nq = _dot(h, w_ref[:, a:a + NA_W])
        na_ref[rows, :NA_W] = (_per_head_rms(nq, na_gq_ref[...], NA_DH) * (NA_DH ** -0.5)).astype(BF16)
        nk = _dot(h, w_ref[:, a + NA_W:a + 2 * NA_W])
        na_ref[rows, NA_W:2 * NA_W] = _per_head_rms(nk, na_gk_ref[...], NA_DH).astype(BF16)
        na_ref[rows, 2 * NA_W:] = _dot(h, w_ref[:, a + 2 * NA_W:a + 3 * NA_W]).astype(BF16)
        cq = _dot(h, w_ref[:, SEG_CQ[0]:SEG_CQ[1]])
        cq_ref[rows, :] = (_per_head_rms(cq, mem_gq_ref[...], MEM_DH) * (MEM_DH ** -0.5)).astype(BF16)


def _inproj(x_src, g, w, na_gq, na_gk, mem_gq, layer, tm):
    n = x_src.shape[0]
    outs = [(SEG_QK, F32), (SEG_V, F32), (SEG_O, F32), (SEG_G, F32), (SEG_NA, BF16), (SEG_CQ, BF16)]
    per_layer_row = lambda w_: pl.BlockSpec((None, 1, w_), lambda i: (layer, 0, 0))
    return pl.pallas_call(
        _inproj_kernel,
        grid=(n // tm,),
        in_specs=[pl.BlockSpec((tm, D_MODEL), lambda i: (i, 0)),
                  per_layer_row(D_MODEL),
                  pl.BlockSpec((None, D_MODEL, PROJ_W), lambda i: (layer, 0, 0)),
                  per_layer_row(NA_W), per_layer_row(NA_W), per_layer_row(MEM_W)],
        out_specs=[pl.BlockSpec((tm, s[1] - s[0]), lambda i: (i, 0)) for s, _ in outs],
        out_shape=[jax.ShapeDtypeStruct((n, s[1] - s[0]), dt) for s, dt in outs],
        compiler_params=_cparams(1),
        name="in_proj",
    )(x_src, g, w, na_gq, na_gk, mem_gq)


HALO = 8


def _log_sigmoid(x):
    return -(jnp.maximum(-x, 0.0) + jnp.log(1.0 + jnp.exp(-jnp.abs(x))))


def _qk_conv_kernel(qk_ref, prev_ref, next_ref, cw_ref, q_ref, k_ref):
    i = pl.program_id(1)
    rows = qk_ref.shape[0]
    prev_on = jnp.where(i > 0, 1.0, 0.0)
    next_on = jnp.where(i < pl.num_programs(1) - 1, 1.0, 0.0)
    ext = jnp.concatenate([prev_ref[...] * prev_on, qk_ref[...], next_ref[...] * next_on], axis=0)
    pad = M_CONV // 2
    conv = ext[HALO - pad:HALO - pad + rows] * cw_ref[0:1, :]
    for j in range(1, M_CONV):
        conv = conv + ext[HALO - pad + j:HALO - pad + j + rows] * cw_ref[j:j + 1, :]
    qk = conv * jax.nn.sigmoid(conv)
    q_ref[...] = (qk[:, :QK_W // 2] * (M_DQK ** -0.5)).astype(BF16)
    k_ref[...] = qk[:, QK_W // 2:]


def _qk_conv(qk, conv_w, layer, bsz, seq, rows):
    hb = rows // HALO
    n_halo = seq // HALO
    return pl.pallas_call(
        _qk_conv_kernel,
        grid=(bsz, seq // rows),
        in_specs=[pl.BlockSpec((None, rows, QK_W), lambda b, i: (b, i, 0)),
                  pl.BlockSpec((None, HALO, QK_W), lambda b, i: (b, jnp.maximum(i * hb - 1, 0), 0)),
                  pl.BlockSpec((None, HALO, QK_W), lambda b, i: (b, jnp.minimum((i + 1) * hb, n_halo - 1), 0)),
                  pl.BlockSpec((None, M_CONV, QK_W), lambda b, i: (layer, 0, 0))],
        out_specs=[pl.BlockSpec((None, rows, QK_W // 2), lambda b, i: (b, i, 0))] * 2,
        out_shape=[jax.ShapeDtypeStruct((bsz, seq, QK_W // 2), BF16),
                   jax.ShapeDtypeStruct((bsz, seq, QK_W // 2), F32)],
        compiler_params=_cparams(2),
        name="mlstm_qk_conv",
    )(qk, qk, qk, conv_w)


def _lane_broadcast_columns(x, first, count):
    r = lax.broadcasted_iota(jnp.int32, (LANES, count * LANES), 0)
    c = lax.broadcasted_iota(jnp.int32, (LANES, count * LANES), 1)
    pick = jnp.where(r == first + c // LANES, 1.0, 0.0).astype(BF16)
    return _dot_exact_rhs(x, pick)


def _mlstm_direction(reverse, q_ref, k_ref, v_ref, g_ref, bg_ref, h_ref, c_ref, m_ref):
    L = M_CHUNK
    gates = g_ref[...] + bg_ref[...]
    log_f = _log_sigmoid(gates)
    t_i = lax.broadcasted_iota(jnp.int32, (L, L), 0)
    s_i = lax.broadcasted_iota(jnp.int32, (L, L), 1)
    visible = (s_i >= t_i) if reverse else (s_i <= t_i)
    b_all = _masked_prefix_sum(visible, log_f)
    off = 2 * M_HEADS if reverse else 0
    last = 0 if reverse else L - 1
    b_cols = _lane_broadcast_columns(b_all, off + M_HEADS, M_HEADS)
    li_cols = _lane_broadcast_columns(gates, off, M_HEADS)
    row_term_t = (gates - pltpu.roll(b_all, LANES - M_HEADS, 1)).T
    ones_b = jnp.ones((L, M_DV), BF16)
    heads_per_block = LANES // M_DQK

    for h in range(M_HEADS):
        blk = slice((h // heads_per_block) * LANES, (h // heads_per_block + 1) * LANES)
        hm = _head_mask(LANES, M_DQK, h % heads_per_block)
        hs = slice(h * LANES, (h + 1) * LANES)
        sidx = (M_HEADS if reverse else 0) + h
        b_col = b_cols[:, hs]
        m_prev = m_ref[sidx:sidx + 1, :]
        g_tot = b_col[last:last + 1, :]
        c_prev_b = c_ref[sidx].astype(BF16)
        q_hb = jnp.where(hm, q_ref[:, blk], jnp.zeros((), BF16))
        k_blk = k_ref[:, blk]
        v_hb = v_ref[:, h * M_DV:(h + 1) * M_DV].astype(BF16)

        d_log = jnp.where(visible, b_col + row_term_t[off + h:off + h + 1, :], -jnp.inf)
        inter = b_col + m_prev
        m_t = jnp.maximum(inter, jnp.max(d_log, axis=-1, keepdims=True))
        s_inter = jnp.exp(inter - m_t)
        scores = _dot_nt(q_hb, k_blk.astype(BF16)) * jnp.exp(d_log - m_t)
        qcn = _dot(q_hb, c_prev_b)
        num = s_inter * qcn[:, :M_DV] + _dot(scores.astype(BF16), v_hb)
        den = s_inter * qcn[:, M_DV:] + jnp.sum(scores, axis=-1, keepdims=True)
        h_ref[:, h * M_DV:(h + 1) * M_DV] = num / jnp.maximum(jnp.abs(den), jnp.exp(-m_t))

        w_log = g_tot - b_col + li_cols[:, hs]
        a = jnp.max(w_log, axis=0, keepdims=True)
        kw_t = (k_blk * jnp.exp(w_log - a)).T.astype(BF16)
        kvn = _dot(kw_t, jnp.concatenate([v_hb, ones_b], axis=1))
        m_new = jnp.maximum(g_tot + m_prev, a)
        s_prev = jnp.exp(g_tot + m_prev - m_new)
        s_cur = jnp.exp(a - m_new)
        two = lambda r: jnp.concatenate([r, r], axis=1)
        c_ref[sidx] = two(s_prev) * c_ref[sidx] + two(s_cur) * kvn
        m_ref[sidx:sidx + 1, :] = m_new


def _masked_prefix_sum(visible, x):
    tri = jnp.where(visible, 1.0, 0.0).astype(BF16)
    x0, x1, x2 = _split_bf16(x, 3)
    return _dot(tri, x0) + _dot(tri, x1) + _dot(tri, x2)


def _mlstm_kernel(qf, kf, vf, gf, qb, kb, vb, gb, bg_ref, hf_ref, hb_ref, c_ref, m_ref):
    @pl.when(pl.program_id(1) == 0)
    def _():
        c_ref[...] = jnp.zeros_like(c_ref)
        m_ref[...] = jnp.zeros_like(m_ref)

    _mlstm_direction(False, qf, kf, vf, gf, bg_ref, hf_ref, c_ref, m_ref)
    _mlstm_direction(True, qb, kb, vb, gb, bg_ref, hb_ref, c_ref, m_ref)


def _mlstm(qk, v, gt, conv_w, b_gate, layer, bsz, seq):
    L = M_CHUNK
    nc = seq // L
    q_b, k = _qk_conv(qk.reshape(bsz, seq, QK_W), conv_w, layer, bsz, seq, 4 * L)
    v3 = v.reshape(bsz, seq, V_W)
    g3 = gt.reshape(bsz, seq, LANES)

    def specs(cidx):
        chunk = lambda w_: pl.BlockSpec((None, L, w_), lambda b, c: (b, cidx(c), 0))
        return [chunk(QK_W // 2), chunk(QK_W // 2), chunk(V_W), chunk(LANES)]

    fw = lambda c: c
    bw = lambda c: nc - 1 - c
    h_fw, h_bw = pl.pallas_call(
        _mlstm_kernel,
        grid=(bsz, nc),
        in_specs=specs(fw) + specs(bw) + [pl.BlockSpec((None, 1, LANES), lambda b, c: (layer, 0, 0))],
        out_specs=[pl.BlockSpec((None, L, V_W), lambda b, c: (b, c, 0)),
                   pl.BlockSpec((None, L, V_W), lambda b, c: (b, nc - 1 - c, 0))],
        out_shape=[jax.ShapeDtypeStruct((bsz, seq, V_W), F32)] * 2,
        scratch_shapes=[pltpu.VMEM((2 * M_HEADS, LANES, 2 * M_DV), F32),
                        pltpu.VMEM((2 * M_HEADS, LANES), F32)],
        compiler_params=_cparams(2),
        name="mlstm",
    )(q_b, k, v3, g3, q_b, k, v3, g3, b_gate)
    return h_fw.reshape(bsz * seq, V_W), h_bw.reshape(bsz * seq, V_W)


def _na_kernel(q_ref, k0, k1, k2, v0, v1, v2, bias_ref, o_ref):
    q = q_ref[...]
    k = jnp.concatenate([k0[...], k1[...], k2[...]], axis=0)
    v = jnp.concatenate([v0[...], v1[...], v2[...]], axis=0)
    zero = jnp.zeros((), q.dtype)
    acc = jnp.zeros(o_ref.shape, F32)
    for h in range(NA_HEADS):
        hm = _head_mask(NA_W, NA_DH, h)
        s = _dot_nt(jnp.where(hm, q, zero), k) + bias_ref[h]
        p = _softmax_rows(s).astype(BF16)
        acc = acc + _dot(p, jnp.where(hm, v, zero))
    o_ref[...] = acc


def _na_bias_selectors(rows):
    nj = rows // NA_QROWS
    qi = np.arange(NA_QROWS)
    ki = np.arange(NA_KROWS)
    row_sel = np.zeros((3, NA_QROWS, NA_KROWS, 2 * NA_KH - 1), np.float32)
    row_ok = np.zeros((3, NA_QROWS, NA_KROWS), bool)
    for pat, j in enumerate((0, 1, nj - 1)):
        ks = min(max(NA_QROWS * j - NA_KH // 2, 0), rows - NA_KROWS)
        r = NA_QROWS * j + qi
        rs = np.clip(r - NA_KH // 2, 0, rows - NA_KH)
        krow = ks + ki
        ok = (krow[None, :] >= rs[:, None]) & (krow[None, :] < rs[:, None] + NA_KH)
        dr = krow[None, :] - r[:, None] + NA_KH - 1
        for a in range(NA_QROWS):
            for c in range(NA_KROWS):
                if ok[a, c]:
                    row_sel[pat, a, c, dr[a, c]] = 1.0
        row_ok[pat] = ok
    qc = np.arange(GRID_W)
    cs = np.clip(qc - NA_KW // 2, 0, GRID_W - NA_KW)
    col_ok = (qc[None, :] >= cs[:, None]) & (qc[None, :] < cs[:, None] + NA_KW)
    dc = qc[None, :] - qc[:, None] + NA_KW - 1
    col_sel = np.zeros((2 * NA_KW - 1, GRID_W, GRID_W), np.float32)
    for a in range(GRID_W):
        for c in range(GRID_W):
            if col_ok[a, c]:
                col_sel[dc[a, c], a, c] = 1.0
    ok = row_ok[:, :, None, :, None] & col_ok[None, None, :, None, :]
    return (row_sel.reshape(3 * NA_QROWS * NA_KROWS, -1), col_sel.reshape(2 * NA_KW - 1, -1),
            ok.reshape(3, NA_QROWS * GRID_W, NA_KROWS * GRID_W))


def _na_bias_table(rpb, rows):
    row_sel, col_sel, ok = _na_bias_selectors(rows)
    hi = lax.Precision.HIGHEST
    picked_rows = jnp.einsum('gr,hrd->hgd', row_sel, rpb, precision=hi)
    band = jnp.einsum('hgd,dn->hgn', picked_rows, col_sel, precision=hi)
    band = band.reshape(-1, 3, NA_QROWS, NA_KROWS, GRID_W, GRID_W).transpose(0, 1, 2, 4, 3, 5)
    return jnp.where(ok[None], band.reshape(-1, 3, NA_QROWS * GRID_W, NA_KROWS * GRID_W), NEG_BIG)


def _neighbourhood_attention(na, bias, layer, bsz, seq):
    rows = seq // GRID_W
    nj = rows // NA_QROWS
    qn = NA_QROWS * GRID_W
    kn = NA_KROWS * GRID_W
    na3 = na.reshape(bsz, seq, 3 * NA_W)
    kstart = lambda j: jnp.clip(j - 1, 0, nj - 3)
    pattern = lambda j: jnp.minimum(j, 1) + jnp.maximum(j - (nj - 2), 0)
    kv_specs = [pl.BlockSpec((None, qn, NA_W), functools.partial(lambda j, b, i, part: (b, kstart(j) + i, part),
                                                                 i=i, part=part))
                for part in (1, 2) for i in range(3)]
    out = pl.pallas_call(
        _na_kernel,
        grid=(nj, bsz),
        in_specs=[pl.BlockSpec((None, qn, NA_W), lambda j, b: (b, j, 0))] + kv_specs + [
            pl.BlockSpec((None, NA_HEADS, None, qn, kn), lambda j, b: (layer, 0, pattern(j), 0, 0))],
        out_specs=pl.BlockSpec((None, qn, NA_W), lambda j, b: (b, j, 0)),
        out_shape=jax.ShapeDtypeStruct((bsz, seq, NA_W), F32),
        compiler_params=_cparams(2),
        name="neighbourhood_attn",
    )(na3, na3, na3, na3, na3, na3, na3, bias)
    return out.reshape(bsz * seq, NA_W)


def _mem_kv_kernel(mem_ref, g_ref, w_ref, gk_ref, k_ref, v_ref):
    h = _rms_rows(mem_ref[...], g_ref[...]).astype(BF16)
    kv = _dot(h, w_ref[...])
    k_ref[...] = _per_head_rms(kv[:, :MEM_W], gk_ref[...], MEM_DH).astype(BF16)
    v_ref[...] = kv[:, MEM_W:].astype(BF16)


def _mem_kv(mem, g_mem, w_kv, gk, layer):
    bsz, n_mem, _ = mem.shape
    return pl.pallas_call(
        _mem_kv_kernel,
        grid=(bsz,),
        in_specs=[pl.BlockSpec((None, n_mem, D_MODEL), lambda b: (b, 0, 0)),
                  pl.BlockSpec((None, 1, D_MODEL), lambda b: (layer, 0, 0)),
                  pl.BlockSpec((None, D_MODEL, 2 * MEM_W), lambda b: (layer, 0, 0)),
                  pl.BlockSpec((None, 1, MEM_W), lambda b: (layer, 0, 0))],
        out_specs=[pl.BlockSpec((None, n_mem, MEM_W), lambda b: (b, 0, 0))] * 2,
        out_shape=[jax.ShapeDtypeStruct((bsz, n_mem, MEM_W), BF16)] * 2,
        compiler_params=_cparams(1),
        name="mem_kv",
    )(mem, g_mem, w_kv, gk)


def _mem_attn_kernel(q_ref, k_ref, v_ref, o_ref):
    q = q_ref[...]
    k_b = k_ref[...]
    v_b = v_ref[...]
    zero = jnp.zeros((), q.dtype)
    acc = jnp.zeros(o_ref.shape, F32)
    for h in range(MEM_HEADS):
        hm = _head_mask(MEM_W, MEM_DH, h)
        p = _softmax_rows(_dot_nt(jnp.where(hm, q, zero), k_b)).astype(BF16)
        acc = acc + _dot(p, jnp.where(hm, v_b, zero))
    o_ref[...] = acc


def _mem_attention(cq, k_m, v_m, bsz, seq, tq):
    n_mem = k_m.shape[1]
    out = pl.pallas_call(
        _mem_attn_kernel,
        grid=(bsz, seq // tq),
        in_specs=[pl.BlockSpec((None, tq, MEM_W), lambda b, i: (b, i, 0)),
                  pl.BlockSpec((None, n_mem, MEM_W), lambda b, i: (b, 0, 0)),
                  pl.BlockSpec((None, n_mem, MEM_W), lambda b, i: (b, 0, 0))],
        out_specs=pl.BlockSpec((None, tq, MEM_W), lambda b, i: (b, i, 0)),
        out_shape=jax.ShapeDtypeStruct((bsz, seq, MEM_W), F32),
        compiler_params=_cparams(2),
        name="mem_attn",
    )(cq.reshape(bsz, seq, MEM_W), k_m, v_m)
    return out.reshape(bsz * seq, MEM_W)


def _outproj_kernel(hf_ref, hb_ref, o_ref, yn_ref, yc_ref, x_ref, w_ref, gh_ref, gf_ref, wr_ref, xz_ref, afft_ref):
    for rows in _row_groups(x_ref.shape[0], OUT_PROJ_ROW_GROUPS):
        hs = hf_ref[rows, :] + hb_ref[rows, :]
        acc = x_ref[rows, :]
        for h in range(M_HEADS):
            sl = slice(h * M_DV, (h + 1) * M_DV)
            y = jax.nn.sigmoid(o_ref[rows, sl]) * _rms_rows(hs[:, sl], gh_ref[:, sl])
            acc = acc + _dot(y.astype(BF16), w_ref[sl, :])
        acc = acc + _dot(yn_ref[rows, :].astype(BF16), w_ref[V_W:V_W + NA_W, :])
        acc = acc + _dot(yc_ref[rows, :].astype(BF16), w_ref[V_W + NA_W:, :])
        xz_ref[rows, XZ_X[0]:XZ_X[1]] = acc
        h2 = _rms_rows(acc, gf_ref[...])
        xz_ref[rows, XZ_H[0]:XZ_H[1]] = h2
        h2_hi, h2_lo = _split_bf16(h2, 2)
        part = _dot(h2_hi, wr_ref[...]) + _dot(h2_lo, wr_ref[...])
        logits = part + pltpu.roll(part, LANES - N_EXPERTS, 1) + pltpu.roll(part, LANES - 2 * N_EXPERTS, 1)
        lane = lax.broadcasted_iota(jnp.int32, logits.shape, 1)
        aff = _softmax_rows(jnp.where(lane < N_EXPERTS, logits, NEG_BIG))
        xz_ref[rows, XZ_A[0]:XZ_A[1]] = aff
        afft_ref[:, rows] = aff.T[:N_EXPERTS, :]


def _outproj(h_fw, h_bw, o, y_n, y_c, x_src, w_out, g_head, g_ffn, w_router_p, layer, tm):
    n = x_src.shape[0]
    row = lambda w_: pl.BlockSpec((tm, w_), lambda i: (i, 0))
    per_layer = lambda a: pl.BlockSpec((None,) + a.shape[1:], lambda i: (layer, 0, 0))
    return pl.pallas_call(
        _outproj_kernel,
        grid=(n // tm,),
        in_specs=[row(V_W), row(V_W), row(V_W), row(NA_W), row(MEM_W), row(D_MODEL), per_layer(w_out),
                  per_layer(g_head), per_layer(g_ffn), per_layer(w_router_p)],
        out_specs=[row(XZ_W), pl.BlockSpec((N_EXPERTS, tm), lambda i: (0, i))],
        out_shape=[jax.ShapeDtypeStruct((n, XZ_W), F32), jax.ShapeDtypeStruct((N_EXPERTS, n), F32)],
        compiler_params=_cparams(1),
        name="out_proj_router",
    )(h_fw, h_bw, o, y_n, y_c, x_src, w_out, g_head, g_ffn, w_router_p)


def _lane_cumsum(mask_f, tri_b):
    n = mask_f.shape[1]
    run = jnp.zeros((mask_f.shape[0], 1), F32)
    parts = []
    for j in range(n // LANES):
        local = _dot(mask_f[:, j * LANES:(j + 1) * LANES].astype(BF16), tri_b) + run
        parts.append(local)
        run = local[:, LANES - 1:LANES]
    return jnp.concatenate(parts, axis=1)


def _select_kernel(aff_ref, idx_ref, csum_ref, cend_ref, *, cap):
    bits = pltpu.bitcast(aff_ref[...], jnp.int32)
    n_e, n_tok = bits.shape
    thr = jnp.zeros((n_e, 1), jnp.int32)
    for bit in range(30, -1, -1):
        cand = thr | (1 << bit)
        cnt = jnp.sum(jnp.where(bits >= cand, 1.0, 0.0), axis=1, keepdims=True)
        thr = jnp.where(cnt >= cap, cand, thr)
    s_i = lax.broadcasted_iota(jnp.int32, (LANES, LANES), 0)
    t_i = lax.broadcasted_iota(jnp.int32, (LANES, LANES), 1)
    tri_b = jnp.where(s_i <= t_i, 1.0, 0.0).astype(BF16)
    gt = bits > thr
    eq_f = jnp.where(bits == thr, 1.0, 0.0)
    need = cap - jnp.sum(jnp.where(gt, 1.0, 0.0), axis=1, keepdims=True)
    eq_rank = _lane_cumsum(eq_f, tri_b)
    sel_f = jnp.where(gt | ((eq_f > 0.5) & (eq_rank <= need)), 1.0, 0.0)
    n_chunks = n_tok // LANES
    assert n_chunks <= LANES and cap <= 16 * 256
    csum_ref[...] = jnp.zeros_like(csum_ref)
    run = jnp.zeros((n_e, 1), F32)
    chunk_lane = lax.broadcasted_iota(jnp.int32, (n_e, LANES), 1)
    cend = jnp.full((n_e, LANES), float(2 * cap), F32)
    for j in range(n_chunks):
        local = _dot(sel_f[:, j * LANES:(j + 1) * LANES].astype(BF16), tri_b) + run
        run = local[:, LANES - 1:LANES]
        cend = jnp.where(chunk_lane == j, run, cend)
        for e in range(n_e):
            csum_ref[e, j:j + 1, :] = local[e:e + 1, :]
    for e in range(n_e):
        cend_ref[e] = cend[e:e + 1, :]
    slot = lax.broadcasted_iota(jnp.int32, (cap, 1), 0).astype(F32)
    lane = lax.broadcasted_iota(jnp.int32, (cap, LANES), 1)

    def per_expert(e, cols):
        counts = csum_ref[e]
        hi = jnp.floor(counts * (1.0 / 16.0))
        lo = counts - 16.0 * hi
        full_chunks = jnp.sum(jnp.where(cend_ref[e] <= slot, 1.0, 0.0), axis=1, keepdims=True)
        pick = jnp.where(lane == full_chunks.astype(jnp.int32), 1.0, 0.0).astype(BF16)
        chunk_counts = 16.0 * _dot(pick, hi.astype(BF16)) + _dot(pick, lo.astype(BF16))
        inside = jnp.sum(jnp.where(chunk_counts <= slot, 1.0, 0.0), axis=1, keepdims=True)
        return jnp.where(lane == e, float(LANES) * full_chunks + inside, cols)

    cols = lax.fori_loop(0, n_e, per_expert, jnp.zeros((cap, LANES), F32))
    idx_ref[...] = cols.T[:n_e, :].astype(jnp.int32) + pl.program_id(0) * n_tok


def _select(aff_t, bsz, seq, cap):
    return pl.pallas_call(
        functools.partial(_select_kernel, cap=cap),
        grid=(bsz,),
        in_specs=[pl.BlockSpec((N_EXPERTS, seq), lambda b: (0, b))],
        out_specs=pl.BlockSpec((None, N_EXPERTS, cap), lambda b: (b, 0, 0)),
        out_shape=jax.ShapeDtypeStruct((bsz, N_EXPERTS, cap), jnp.int32),
        scratch_shapes=[pltpu.VMEM((N_EXPERTS, LANES, LANES), F32), pltpu.VMEM((N_EXPERTS, 1, LANES), F32)],
        compiler_params=_cparams(1),
        name="expert_select",
    )(aff_t)


FF_CHUNK = 512


N_ROWBUF = 3
W_CHUNKS = 8


def _experts_kernel(idx_prev, idx_cur, idx_next, xz_hbm, w1_hbm, w3_hbm, w2_hbm, out_hbm, buf, xb_ref, w1_ref,
                    w3_ref, w2_ref, st1, st3, st2, gsem, ssem, wsem, *, cap, layer, chunks_per_step):
    del xz_hbm
    e, b = pl.program_id(0), pl.program_id(1)
    n_e, n_b = pl.num_programs(0), pl.num_programs(1)
    k = e * n_b + b
    wslot = lax.rem(e, 2)
    up_rows, down_rows = D_MODEL // W_CHUNKS, D_FF // W_CHUNKS

    def weight_chunk_copies(expert, c):
        up = pl.ds(pl.multiple_of(c * up_rows, up_rows), up_rows)
        down = pl.ds(pl.multiple_of(c * down_rows, down_rows), down_rows)
        return (pltpu.make_async_copy(w1_hbm.at[layer, expert, up], st1, wsem.at[0]),
                pltpu.make_async_copy(w3_hbm.at[layer, expert, up], st3, wsem.at[1]),
                pltpu.make_async_copy(w2_hbm.at[layer, expert, down], st2, wsem.at[2]))

    def store_weight_chunk(slot, c):
        up = pl.ds(pl.multiple_of(c * up_rows, up_rows), up_rows)
        down = pl.ds(pl.multiple_of(c * down_rows, down_rows), down_rows)
        w1_ref[slot, up, :] = st1[...].astype(BF16)
        w3_ref[slot, up, :] = st3[...].astype(BF16)
        w2_ref[slot, down, :] = st2[...].astype(BF16)

    @pl.when(k == 0)
    def _():
        def load_chunk(c, carry):
            copies = weight_chunk_copies(0, c)
            for cp in copies:
                cp.start()
            for cp in copies:
                cp.wait()
            store_weight_chunk(0, c)
            return carry
        lax.fori_loop(0, W_CHUNKS, load_chunk, 0)
    slot_cur = lax.rem(k, N_ROWBUF)
    slot_next = lax.rem(k + 1, N_ROWBUF)
    slot_prev = lax.rem(k + 2, N_ROWBUF)

    def fetch_row(idx_ref, slot, i):
        return pltpu.make_async_copy(out_hbm.at[pl.ds(idx_ref[0, i], 1)], buf.at[slot, pl.ds(i, 1)], gsem.at[slot])

    def writeback_row(idx_ref, slot, i):
        return pltpu.make_async_copy(buf.at[slot, pl.ds(i, 1), pl.ds(XZ_X[0], D_MODEL)],
                                     out_hbm.at[pl.ds(idx_ref[0, i], 1), pl.ds(XZ_X[0], D_MODEL)], ssem.at[slot])

    def wait_fetched(slot):
        pltpu.make_async_copy(out_hbm.at[pl.ds(0, cap)], buf.at[slot], gsem.at[slot]).wait()

    def wait_written(slot):
        pltpu.make_async_copy(buf.at[slot, :, pl.ds(XZ_X[0], D_MODEL)],
                              out_hbm.at[pl.ds(0, cap), pl.ds(XZ_X[0], D_MODEL)], ssem.at[slot]).wait()

    def start_all(make_copy):
        def body(i, carry):
            make_copy(i).start()
            return carry
        lax.fori_loop(0, cap, body, 0)

    @pl.when(k == 0)
    def _():
        start_all(lambda i: fetch_row(idx_prev, slot_prev, i))
        wait_fetched(slot_prev)
        start_all(lambda i: fetch_row(idx_cur, slot_cur, i))

    @pl.when(k > 0)
    def _():
        wait_written(slot_next)

    next_chunk = b * chunks_per_step
    stream_on = e + 1 < n_e

    @pl.when(stream_on & (next_chunk < W_CHUNKS))
    def _():
        for cp in weight_chunk_copies(e + 1, next_chunk):
            cp.start()

    wait_fetched(slot_cur)
    lane = lax.broadcasted_iota(jnp.int32, (cap, LANES), 1)
    gate = jnp.sum(jnp.where(lane == e, buf[slot_cur, :, XZ_A[0]:XZ_A[1]], 0.0), axis=1, keepdims=True)
    n_ff = D_FF // FF_CHUNK
    bounds = [0] + [(2 * cap * f) // (n_ff - 1) for f in range(n_ff)]
    xb_ref[...] = buf[slot_cur, :, XZ_H[0]:XZ_H[1]].astype(BF16)
    def start_copies(lo, hi):
        for i in range(lo, hi):
            if i < cap:
                fetch_row(idx_next, slot_next, i).start()
            else:
                writeback_row(idx_prev, slot_prev, i - cap).start()

    for f in range(n_ff):
        xb = xb_ref[...]
        sl = slice(f * FF_CHUNK, (f + 1) * FF_CHUNK)
        start_copies(bounds[f], bounds[f + 1])
        h1 = _dot(xb, w1_ref[wslot, :, sl])
        h3 = _dot(xb, w3_ref[wslot, :, sl])
        hid = (h1 * jax.nn.sigmoid(h1) * h3).astype(BF16)
        buf[slot_cur, :, XZ_X[0]:XZ_X[1]] = (buf[slot_cur, :, XZ_X[0]:XZ_X[1]]
                                             + _dot(hid, w2_ref[wslot, sl, :]) * gate)

    for j in range(chunks_per_step):
        @pl.when(stream_on & (next_chunk + j < W_CHUNKS))
        def _(j=j):
            copies = weight_chunk_copies(e + 1, next_chunk + j)
            if j > 0:
                for cp in copies:
                    cp.start()
            for cp in copies:
                cp.wait()
            store_weight_chunk(1 - wslot, next_chunk + j)

    @pl.when(k == n_e * n_b - 1)
    def _():
        wait_fetched(slot_next)
        wait_written(slot_prev)
        start_all(lambda i: writeback_row(idx_cur, slot_cur, i))
        wait_written(slot_cur)


def _experts(idx, xz, w1, w3, w2, layer, bsz, cap):
    assert bsz >= N_ROWBUF, "rows in flight for neighbouring steps must belong to different sequences"
    idx3 = idx.reshape(bsz * N_EXPERTS, 1, cap)

    def idx_spec(step):
        def index_map(e, b):
            b2 = lax.rem(b + step + bsz, bsz)
            e2 = jnp.clip(e + (b + step + bsz) // bsz - 1, 0, N_EXPERTS - 1)
            return (b2 * N_EXPERTS + e2, 0, 0)
        return pl.BlockSpec((None, 1, cap), index_map, memory_space=pltpu.SMEM)

    any_space = pl.BlockSpec(memory_space=pl.ANY)
    chunks_per_step = -(-W_CHUNKS // bsz)
    return pl.pallas_call(
        functools.partial(_experts_kernel, cap=cap, layer=layer, chunks_per_step=chunks_per_step),
        grid=(N_EXPERTS, bsz),
        in_specs=[idx_spec(-1), idx_spec(0), idx_spec(1), any_space, any_space, any_space, any_space],
        out_specs=any_space,
        out_shape=jax.ShapeDtypeStruct(xz.shape, F32),
        scratch_shapes=[pltpu.VMEM((N_ROWBUF, cap, XZ_W), F32), pltpu.VMEM((cap, D_MODEL), BF16),
                        pltpu.VMEM((2, D_MODEL, D_FF), BF16), pltpu.VMEM((2, D_MODEL, D_FF), BF16),
                        pltpu.VMEM((2, D_FF, D_MODEL), BF16),
                        pltpu.VMEM((D_MODEL // W_CHUNKS, D_FF), F32), pltpu.VMEM((D_MODEL // W_CHUNKS, D_FF), F32),
                        pltpu.VMEM((D_FF // W_CHUNKS, D_MODEL), F32),
                        pltpu.SemaphoreType.DMA((N_ROWBUF,)), pltpu.SemaphoreType.DMA((N_ROWBUF,)),
                        pltpu.SemaphoreType.DMA((3,))],
        input_output_aliases={3: 0},
        compiler_params=_cparams(2),
        name="experts",
    )(idx3, idx3, idx3, xz, w1, w3, w2)


def _repack_w_in(w_in):
    mlstm_w = 2 * M_HEADS * M_DQK + 2 * M_HEADS * M_DV
    pad = jnp.zeros(w_in.shape[:2] + (LANES - N_GATES,), w_in.dtype)
    return jnp.concatenate([w_in[..., :mlstm_w + N_GATES], pad, w_in[..., mlstm_w + N_GATES:]], axis=-1).astype(BF16)


def kernel(x, mem, g_mix, w_in, b_gates, conv_qk, g_mlstm_head, na_gq, na_gk, na_rpb, g_mem, w_mem_kv, mem_gq,
           mem_gk, w_out, g_ffn, w_router, w1, w3, w2):
    bsz, seq, _ = x.shape
    depth = w_in.shape[0]
    rows = seq // GRID_W
    cap = EC_CAPACITY * seq // N_EXPERTS
    tm = 512

    w_in_b = _repack_w_in(w_in)
    w_out_b = w_out.astype(BF16)
    w_kv_b = w_mem_kv.astype(BF16)
    b_gate_p = jnp.pad(b_gates, ((0, 0), (0, LANES - N_GATES)))[:, None, :]
    w_router_p = jnp.pad(jnp.concatenate(_split_bf16(w_router, 3), axis=-1),
                         ((0, 0), (0, 0), (0, LANES - 3 * N_EXPERTS)))
    na_bias = jax.vmap(lambda r: _na_bias_table(r, rows))(na_rpb)
    row3 = lambda a: a[:, None, :]
    per_head = lambda a, n_heads: row3(jnp.tile(a, (1, n_heads)))
    g_mix3, g_mem3, g_head3, g_ffn3 = row3(g_mix), row3(g_mem), row3(g_mlstm_head), row3(g_ffn)
    na_gq3, na_gk3 = per_head(na_gq, NA_HEADS), per_head(na_gk, NA_HEADS)
    mem_gq3, mem_gk3 = per_head(mem_gq, MEM_HEADS), per_head(mem_gk, MEM_HEADS)

    xs = x.reshape(bsz * seq, D_MODEL)
    for l in range(depth):
        qk, v, o, gt, na, cq = _inproj(xs, g_mix3, w_in_b, na_gq3, na_gk3, mem_gq3, l, tm * IN_PROJ_ROW_GROUPS)
        h_fw, h_bw = _mlstm(qk, v, gt, conv_qk, b_gate_p, l, bsz, seq)
        y_n = _neighbourhood_attention(na, na_bias, l, bsz, seq)
        k_m, v_m = _mem_kv(mem, g_mem3, w_kv_b, mem_gk3, l)
        y_c = _mem_attention(cq, k_m, v_m, bsz, seq, tm)
        xz, aff_t = _outproj(h_fw, h_bw, o, y_n, y_c, xs, w_out_b, g_head3, g_ffn3, w_router_p, l,
                             tm * OUT_PROJ_ROW_GROUPS // 2)
        idx = _select(aff_t, bsz, seq, cap)
        xs = _experts(idx, xz, w1, w3, w2, l, bsz, cap)
    return xs[:, XZ_X[0]:XZ_X[1]].reshape(bsz, seq, D_MODEL)
```

```python
import functools

import numpy as np
import jax
import jax.numpy as jnp
from jax import lax
from jax.experimental import pallas as pl
from jax.experimental.pallas import tpu as pltpu

F32 = jnp.float32
BF16 = jnp.bfloat16

D_MODEL = 1024
GRID_W = 64
M_HEADS = 4
M_DQK = 64
M_DV = 128
M_CHUNK = 128
M_CONV = 5
NA_HEADS = 4
NA_DH = 64
NA_KH = 8
NA_KW = 16
MEM_HEADS = 4
MEM_DH = 64
N_EXPERTS = 16
EC_CAPACITY = 2
D_FF = 2048
N_GATES = 4 * M_HEADS
EPS = 1e-6

LANES = 128
QK_W = 2 * M_HEADS * M_DQK
V_W = M_HEADS * M_DV
NA_W = NA_HEADS * NA_DH
MEM_W = MEM_HEADS * MEM_DH
SEG_QK = (0, QK_W)
SEG_V = (SEG_QK[1], SEG_QK[1] + V_W)
SEG_O = (SEG_V[1], SEG_V[1] + V_W)
SEG_G = (SEG_O[1], SEG_O[1] + LANES)
SEG_NA = (SEG_G[1], SEG_G[1] + 3 * NA_W)
SEG_CQ = (SEG_NA[1], SEG_NA[1] + MEM_W)
PROJ_W = SEG_CQ[1]
XZ_X = (0, D_MODEL)
XZ_H = (XZ_X[1], XZ_X[1] + D_MODEL)
XZ_A = (XZ_H[1], XZ_H[1] + LANES)
XZ_W = XZ_A[1]
NA_QROWS = 4
NA_KROWS = NA_QROWS + NA_KH
NEG_BIG = -1e30
VMEM_LIMIT = 56 * 1024 * 1024
IN_PROJ_ROW_GROUPS = 2
OUT_PROJ_ROW_GROUPS = 4


def _cparams(n_axes):
    return pltpu.CompilerParams(dimension_semantics=("arbitrary",) * n_axes, vmem_limit_bytes=VMEM_LIMIT)


def _dot(a, b):
    return jnp.dot(a, b, preferred_element_type=F32)


def _dot_nt(a, b):
    return lax.dot_general(a, b, (((1,), (1,)), ((), ())), preferred_element_type=F32)


def _dot_exact_rhs(a, b_bf16, terms=3):
    return sum(_dot(part, b_bf16) for part in _split_bf16(a, terms))


def _split_bf16(w, terms):
    parts, r = [], w
    for _ in range(terms):
        p = r.astype(BF16)
        parts.append(p)
        r = r - p.astype(F32)
    return parts


def _head_mask(width, head_dim, h):
    lane = lax.broadcasted_iota(jnp.int32, (1, width), 1)
    return (lane // head_dim) == h


def _same_head_matrix(width, head_dim):
    r = lax.broadcasted_iota(jnp.int32, (width, width), 0) // head_dim
    c = lax.broadcasted_iota(jnp.int32, (width, width), 1) // head_dim
    return jnp.where(r == c, 1.0, 0.0).astype(BF16)


def _per_head_rms(x, gain, head_dim):
    ss = _dot_exact_rhs(x * x, _same_head_matrix(x.shape[1], head_dim), terms=2)
    return x * lax.rsqrt(ss * (1.0 / head_dim) + EPS) * gain


def _rms_rows(x, gain):
    ms = jnp.mean(x * x, axis=-1, keepdims=True)
    return x * lax.rsqrt(ms + EPS) * gain


def _row_groups(n_rows, groups):
    step = n_rows // groups
    return [slice(r * step, (r + 1) * step) for r in range(groups)]


def _softmax_rows(s):
    m = jnp.max(s, axis=-1, keepdims=True)
    e = jnp.exp(s - m)
    return e * (1.0 / jnp.sum(e, axis=-1, keepdims=True))


HALO = 8


def _inproj_kernel(x_ref, xprev_ref, xnext_ref, g_ref, w_ref, cw_ref, na_gq_ref, na_gk_ref, mem_gq_ref, q_ref, k_ref,
                   v_ref, o_ref, gt_ref, na_ref, cq_ref, qk_s, *, tiles_per_seq):
    tm = x_ref.shape[0]
    t = lax.rem(pl.program_id(0), tiles_per_seq)
    halo_on = jnp.concatenate([jnp.full((HALO, 1), jnp.where(t > 0, 1.0, 0.0), F32),
                               jnp.full((HALO, 1), jnp.where(t < tiles_per_seq - 1, 1.0, 0.0), F32)], axis=0)
    x_halo = jnp.concatenate([xprev_ref[...], xnext_ref[...]], axis=0)
    qk_halo = _dot(_rms_rows(x_halo, g_ref[...]).astype(BF16), w_ref[:, SEG_QK[0]:SEG_QK[1]]) * halo_on
    qk_s[0:HALO, :] = qk_halo[0:HALO]
    qk_s[HALO + tm:, :] = qk_halo[HALO:]
    pad = M_CONV // 2

    def conv_rows(rows):
        n_rows = rows.stop - rows.start
        first = HALO - pad + rows.start
        conv = qk_s[first:first + n_rows, :] * cw_ref[0:1, :]
        for j in range(1, M_CONV):
            conv = conv + qk_s[first + j:first + j + n_rows, :] * cw_ref[j:j + 1, :]
        qk = conv * jax.nn.sigmoid(conv)
        q_ref[rows, :] = (qk[:, :QK_W // 2] * (M_DQK ** -0.5)).astype(BF16)
        k_ref[rows, :] = qk[:, QK_W // 2:]

    groups = _row_groups(tm, IN_PROJ_ROW_GROUPS)
    for gi, rows in enumerate(groups):
        h = _rms_rows(x_ref[rows, :], g_ref[...]).astype(BF16)
        qk_s[HALO + rows.start:HALO + rows.stop, :] = _dot(h, w_ref[:, SEG_QK[0]:SEG_QK[1]])
        if gi > 0:
            conv_rows(groups[gi - 1])
        for seg, out in ((SEG_V, v_ref), (SEG_O, o_ref), (SEG_G, gt_ref)):
            out[rows, :] = _dot(h, w_ref[:, seg[0]:seg[1]])
        a = SEG_NA[0]
        nq = _dot(h, w_ref[:, a:a + NA_W])
        na_ref[rows, :NA_W] = (_per_head_rms(nq, na_gq_ref[...], NA_DH) * (NA_DH ** -0.5)).astype(BF16)
        nk = _dot(h, w_ref[:, a + NA_W:a + 2 * NA_W])
        na_ref[rows, NA_W:2 * NA_W] = _per_head_rms(nk, na_gk_ref[...], NA_DH).astype(BF16)
        na_ref[rows, 2 * NA_W:] = _dot(h, w_ref[:, a + 2 * NA_W:a + 3 * NA_W]).astype(BF16)
        cq = _dot(h, w_ref[:, SEG_CQ[0]:SEG_CQ[1]])
        cq_ref[rows, :] = (_per_head_rms(cq, mem_gq_ref[...], MEM_DH) * (MEM_DH ** -0.5)).astype(BF16)
    conv_rows(groups[-1])


def _inproj(x_src, g, w, conv_w, na_gq, na_gk, mem_gq, layer, seq, tm):
    n = x_src.shape[0]
    assert seq % tm == 0, "a tile never straddles two sequences"
    outs = [(QK_W // 2, BF16), (QK_W // 2, F32), (V_W, F32), (V_W, F32), (LANES, F32), (3 * NA_W, BF16), (MEM_W, BF16)]
    per_layer_row = lambda w_: pl.BlockSpec((None, 1, w_), lambda i: (layer, 0, 0))
    hb = tm // HALO
    return pl.pallas_call(
        functools.partial(_inproj_kernel, tiles_per_seq=seq // tm),
        grid=(n // tm,),
        in_specs=[pl.BlockSpec((tm, D_MODEL), lambda i: (i, 0)),
                  pl.BlockSpec((HALO, D_MODEL), lambda i: (jnp.maximum(i * hb - 1, 0), 0)),
                  pl.BlockSpec((HALO, D_MODEL), lambda i: (jnp.minimum((i + 1) * hb, n // HALO - 1), 0)),
                  per_layer_row(D_MODEL),
                  pl.BlockSpec((None, D_MODEL, PROJ_W), lambda i: (layer, 0, 0)),
                  pl.BlockSpec((None, M_CONV, QK_W), lambda i: (layer, 0, 0)),
                  per_layer_row(NA_W), per_layer_row(NA_W), per_layer_row(MEM_W)],
        out_specs=[pl.BlockSpec((tm, w_), lambda i: (i, 0)) for w_, _ in outs],
        out_shape=[jax.ShapeDtypeStruct((n, w_), dt) for w_, dt in outs],
        scratch_shapes=[pltpu.VMEM((tm + 2 * HALO, QK_W), F32)],
        compiler_params=_cparams(1),
        name="in_proj",
    )(x_src, x_src, x_src, g, w, conv_w, na_gq, na_gk, mem_gq)


def _log_sigmoid(x):
    return -(jnp.maximum(-x, 0.0) + jnp.log(1.0 + jnp.exp(-jnp.abs(x))))


def _lane_broadcast_columns(x, first, count):
    terms = 2
    r = lax.broadcasted_iota(jnp.int32, (terms * LANES, count * LANES), 0)
    c = lax.broadcasted_iota(jnp.int32, (terms * LANES, count * LANES), 1)
    pick = jnp.where(r % LANES == first + c // LANES, 1.0, 0.0).astype(BF16)
    return _dot(jnp.concatenate(_split_bf16(x, terms), axis=1), pick)


def _mlstm_direction(reverse, q_ref, k_ref, v_ref, g_ref, bg_ref, h_ref, c_ref, m_ref):
    L = M_CHUNK
    gates = g_ref[...] + bg_ref[...]
    log_f = _log_sigmoid(gates)
    t_i = lax.broadcasted_iota(jnp.int32, (L, L), 0)
    s_i = lax.broadcasted_iota(jnp.int32, (L, L), 1)
    visible = (s_i >= t_i) if reverse else (s_i <= t_i)
    b_all = _masked_prefix_sum(visible, log_f)
    off = 2 * M_HEADS if reverse else 0
    last = 0 if reverse else L - 1
    b_cols = _lane_broadcast_columns(b_all, off + M_HEADS, M_HEADS)
    li_cols = _lane_broadcast_columns(gates, off, M_HEADS)
    row_term_t = (gates - pltpu.roll(b_all, LANES - M_HEADS, 1)).T
    ones_b = jnp.ones((L, M_DV), BF16)
    heads_per_block = LANES // M_DQK

    for h in range(M_HEADS):
        blk = slice((h // heads_per_block) * LANES, (h // heads_per_block + 1) * LANES)
        hm = _head_mask(LANES, M_DQK, h % heads_per_block)
        hs = slice(h * LANES, (h + 1) * LANES)
        sidx = (M_HEADS if reverse else 0) + h
        b_col = b_cols[:, hs]
        m_prev = m_ref[sidx:sidx + 1, :]
        g_tot = b_col[last:last + 1, :]
        c_prev_b = c_ref[sidx].astype(BF16)
        q_hb = jnp.where(hm, q_ref[:, blk], jnp.zeros((), BF16))
        k_blk = k_ref[:, blk]
        v_hb = v_ref[:, h * M_DV:(h + 1) * M_DV].astype(BF16)

        d_log = jnp.where(visible, b_col + row_term_t[off + h:off + h + 1, :], -jnp.inf)
        inter = b_col + m_prev
        m_t = jnp.maximum(inter, jnp.max(d_log, axis=-1, keepdims=True))
        s_inter = jnp.exp(inter - m_t)
        scores = _dot_nt(q_hb, k_blk.astype(BF16)) * jnp.exp(d_log - m_t)
        qcn = _dot(q_hb, c_prev_b)
        num = s_inter * qcn[:, :M_DV] + _dot(scores.astype(BF16), v_hb)
        den = s_inter * qcn[:, M_DV:] + jnp.sum(scores, axis=-1, keepdims=True)
        h_ref[:, h * M_DV:(h + 1) * M_DV] = num / jnp.maximum(jnp.abs(den), jnp.exp(-m_t))

        w_log = g_tot - b_col + li_cols[:, hs]
        a = jnp.max(w_log, axis=0, keepdims=True)
        kw_t = (k_blk * jnp.exp(w_log - a)).T.astype(BF16)
        kvn = _dot(kw_t, jnp.concatenate([v_hb, ones_b], axis=1))
        m_new = jnp.maximum(g_tot + m_prev, a)
        s_prev = jnp.exp(g_tot + m_prev - m_new)
        s_cur = jnp.exp(a - m_new)
        two = lambda r: jnp.concatenate([r, r], axis=1)
        c_ref[sidx] = two(s_prev) * c_ref[sidx] + two(s_cur) * kvn
        m_ref[sidx:sidx + 1, :] = m_new


def _masked_prefix_sum(visible, x):
    tri = jnp.where(visible, 1.0, 0.0).astype(BF16)
    x0, x1, x2 = _split_bf16(x, 3)
    return _dot(tri, x0) + _dot(tri, x1) + _dot(tri, x2)


def _mlstm_kernel(qf, kf, vf, gf, qb, kb, vb, gb, bg_ref, hf_ref, hb_ref, c_ref, m_ref):
    @pl.when(pl.program_id(1) == 0)
    def _():
        c_ref[...] = jnp.zeros_like(c_ref)
        m_ref[...] = jnp.zeros_like(m_ref)

    _mlstm_direction(False, qf, kf, vf, gf, bg_ref, hf_ref, c_ref, m_ref)
    _mlstm_direction(True, qb, kb, vb, gb, bg_ref, hb_ref, c_ref, m_ref)


def _mlstm(q_b, k, v, gt, b_gate, layer, bsz, seq):
    L = M_CHUNK
    nc = seq // L
    q_b = q_b.reshape(bsz, seq, QK_W // 2)
    k = k.reshape(bsz, seq, QK_W // 2)
    v3 = v.reshape(bsz, seq, V_W)
    g3 = gt.reshape(bsz, seq, LANES)

    def specs(cidx):
        chunk = lambda w_: pl.BlockSpec((None, L, w_), lambda b, c: (b, cidx(c), 0))
        return [chunk(QK_W // 2), chunk(QK_W // 2), chunk(V_W), chunk(LANES)]

    fw = lambda c: c
    bw = lambda c: nc - 1 - c
    h_fw, h_bw = pl.pallas_call(
        _mlstm_kernel,
        grid=(bsz, nc),
        in_specs=specs(fw) + specs(bw) + [pl.BlockSpec((None, 1, LANES), lambda b, c: (layer, 0, 0))],
        out_specs=[pl.BlockSpec((None, L, V_W), lambda b, c: (b, c, 0)),
                   pl.BlockSpec((None, L, V_W), lambda b, c: (b, nc - 1 - c, 0))],
        out_shape=[jax.ShapeDtypeStruct((bsz, seq, V_W), F32)] * 2,
        scratch_shapes=[pltpu.VMEM((2 * M_HEADS, LANES, 2 * M_DV), F32),
                        pltpu.VMEM((2 * M_HEADS, LANES), F32)],
        compiler_params=_cparams(2),
        name="mlstm",
    )(q_b, k, v3, g3, q_b, k, v3, g3, b_gate)
    return h_fw.reshape(bsz * seq, V_W), h_bw.reshape(bsz * seq, V_W)


def _na_kernel(q_ref, k0, k1, k2, v0, v1, v2, bias_ref, o_ref):
    q = q_ref[...]
    k = jnp.concatenate([k0[...], k1[...], k2[...]], axis=0)
    v = jnp.concatenate([v0[...], v1[...], v2[...]], axis=0)
    zero = jnp.zeros((), q.dtype)
    acc = jnp.zeros(o_ref.shape, F32)
    for h in range(NA_HEADS):
        hm = _head_mask(NA_W, NA_DH, h)
        s = _dot_nt(jnp.where(hm, q, zero), k) + bias_ref[h]
        p = _softmax_rows(s).astype(BF16)
        acc = acc + _dot(p, jnp.where(hm, v, zero))
    o_ref[...] = acc


def _na_bias_selectors(rows):
    nj = rows // NA_QROWS
    qi = np.arange(NA_QROWS)
    ki = np.arange(NA_KROWS)
    row_sel = np.zeros((3, NA_QROWS, NA_KROWS, 2 * NA_KH - 1), np.float32)
    row_ok = np.zeros((3, NA_QROWS, NA_KROWS), bool)
    for pat, j in enumerate((0, 1, nj - 1)):
        ks = min(max(NA_QROWS * j - NA_KH // 2, 0), rows - NA_KROWS)
        r = NA_QROWS * j + qi
        rs = np.clip(r - NA_KH // 2, 0, rows - NA_KH)
        krow = ks + ki
        ok = (krow[None, :] >= rs[:, None]) & (krow[None, :] < rs[:, None] + NA_KH)
        dr = krow[None, :] - r[:, None] + NA_KH - 1
        for a in range(NA_QROWS):
            for c in range(NA_KROWS):
                if ok[a, c]:
                    row_sel[pat, a, c, dr[a, c]] = 1.0
        row_ok[pat] = ok
    qc = np.arange(GRID_W)
    cs = np.clip(qc - NA_KW // 2, 0, GRID_W - NA_KW)
    col_ok = (qc[None, :] >= cs[:, None]) & (qc[None, :] < cs[:, None] + NA_KW)
    dc = qc[None, :] - qc[:, None] + NA_KW - 1
    col_sel = np.zeros((2 * NA_KW - 1, GRID_W, GRID_W), np.float32)
    for a in range(GRID_W):
        for c in range(GRID_W):
            if col_ok[a, c]:
                col_sel[dc[a, c], a, c] = 1.0
    ok = row_ok[:, :, None, :, None] & col_ok[None, None, :, None, :]
    return (row_sel.reshape(3 * NA_QROWS * NA_KROWS, -1), col_sel.reshape(2 * NA_KW - 1, -1),
            ok.reshape(3, NA_QROWS * GRID_W, NA_KROWS * GRID_W))


def _na_bias_table(rpb, rows):
    row_sel, col_sel, ok = _na_bias_selectors(rows)
    hi = lax.Precision.HIGHEST
    picked_rows = jnp.einsum('gr,hrd->hgd', row_sel, rpb, precision=hi)
    band = jnp.einsum('hgd,dn->hgn', picked_rows, col_sel, precision=hi)
    band = band.reshape(-1, 3, NA_QROWS, NA_KROWS, GRID_W, GRID_W).transpose(0, 1, 2, 4, 3, 5)
    return jnp.where(ok[None], band.reshape(-1, 3, NA_QROWS * GRID_W, NA_KROWS * GRID_W), NEG_BIG)


def _neighbourhood_attention(na, bias, layer, bsz, seq):
    rows = seq // GRID_W
    nj = rows // NA_QROWS
    qn = NA_QROWS * GRID_W
    kn = NA_KROWS * GRID_W
    na3 = na.reshape(bsz, seq, 3 * NA_W)
    kstart = lambda j: jnp.clip(j - 1, 0, nj - 3)
    pattern = lambda j: jnp.minimum(j, 1) + jnp.maximum(j - (nj - 2), 0)
    kv_specs = [pl.BlockSpec((None, qn, NA_W), functools.partial(lambda j, b, i, part: (b, kstart(j) + i, part),
                                                                 i=i, part=part))
                for part in (1, 2) for i in range(3)]
    out = pl.pallas_call(
        _na_kernel,
        grid=(nj, bsz),
        in_specs=[pl.BlockSpec((None, qn, NA_W), lambda j, b: (b, j, 0))] + kv_specs + [
            pl.BlockSpec((None, NA_HEADS, None, qn, kn), lambda j, b: (layer, 0, pattern(j), 0, 0))],
        out_specs=pl.BlockSpec((None, qn, NA_W), lambda j, b: (b, j, 0)),
        out_shape=jax.ShapeDtypeStruct((bsz, seq, NA_W), F32),
        compiler_params=_cparams(2),
        name="neighbourhood_attn",
    )(na3, na3, na3, na3, na3, na3, na3, bias)
    return out.reshape(bsz * seq, NA_W)


def _mem_kv_kernel(mem_ref, g_ref, w_ref, gk_ref, k_ref, v_ref):
    h = _rms_rows(mem_ref[...], g_ref[...]).astype(BF16)
    kv = _dot(h, w_ref[...])
    k_ref[...] = _per_head_rms(kv[:, :MEM_W], gk_ref[...], MEM_DH).astype(BF16)
    v_ref[...] = kv[:, MEM_W:].astype(BF16)


def _mem_kv(mem, g_mem, w_kv, gk, layer):
    bsz, n_mem, _ = mem.shape
    return pl.pallas_call(
        _mem_kv_kernel,
        grid=(bsz,),
        in_specs=[pl.BlockSpec((None, n_mem, D_MODEL), lambda b: (b, 0, 0)),
                  pl.BlockSpec((None, 1, D_MODEL), lambda b: (layer, 0, 0)),
                  pl.BlockSpec((None, D_MODEL, 2 * MEM_W), lambda b: (layer, 0, 0)),
                  pl.BlockSpec((None, 1, MEM_W), lambda b: (layer, 0, 0))],
        out_specs=[pl.BlockSpec((None, n_mem, MEM_W), lambda b: (b, 0, 0))] * 2,
        out_shape=[jax.ShapeDtypeStruct((bsz, n_mem, MEM_W), BF16)] * 2,
        compiler_params=_cparams(1),
        name="mem_kv",
    )(mem, g_mem, w_kv, gk)


def _mem_attn_kernel(q_ref, k_ref, v_ref, o_ref):
    q = q_ref[...]
    k_b = k_ref[...]
    v_b = v_ref[...]
    zero = jnp.zeros((), q.dtype)
    acc = jnp.zeros(o_ref.shape, F32)
    for h in range(MEM_HEADS):
        hm = _head_mask(MEM_W, MEM_DH, h)
        p = _softmax_rows(_dot_nt(jnp.where(hm, q, zero), k_b)).astype(BF16)
        acc = acc + _dot(p, jnp.where(hm, v_b, zero))
    o_ref[...] = acc


def _mem_attention(cq, k_m, v_m, bsz, seq, tq):
    n_mem = k_m.shape[1]
    out = pl.pallas_call(
        _mem_attn_kernel,
        grid=(bsz, seq // tq),
        in_specs=[pl.BlockSpec((None, tq, MEM_W), lambda b, i: (b, i, 0)),
                  pl.BlockSpec((None, n_mem, MEM_W), lambda b, i: (b, 0, 0)),
                  pl.BlockSpec((None, n_mem, MEM_W), lambda b, i: (b, 0, 0))],
        out_specs=pl.BlockSpec((None, tq, MEM_W), lambda b, i: (b, i, 0)),
        out_shape=jax.ShapeDtypeStruct((bsz, seq, MEM_W), F32),
        compiler_params=_cparams(2),
        name="mem_attn",
    )(cq.reshape(bsz, seq, MEM_W), k_m, v_m)
    return out.reshape(bsz * seq, MEM_W)


def _outproj_kernel(hf_ref, hb_ref, o_ref, yn_ref, yc_ref, x_ref, w_ref, gh_ref, gf_ref, wr_ref, xz_ref, afft_ref):
    for rows in _row_groups(x_ref.shape[0], OUT_PROJ_ROW_GROUPS):
        hs = hf_ref[rows, :] + hb_ref[rows, :]
        acc = x_ref[rows, :]
        for h in range(M_HEADS):
            sl = slice(h * M_DV, (h + 1) * M_DV)
            y = jax.nn.sigmoid(o_ref[rows, sl]) * _rms_rows(hs[:, sl], gh_ref[:, sl])
            acc = acc + _dot(y.astype(BF16), w_ref[sl, :])
        acc = acc + _dot(yn_ref[rows, :].astype(BF16), w_ref[V_W:V_W + NA_W, :])
        acc = acc + _dot(yc_ref[rows, :].astype(BF16), w_ref[V_W + NA_W:, :])
        xz_ref[rows, XZ_X[0]:XZ_X[1]] = acc
        h2 = _rms_rows(acc, gf_ref[...])
        xz_ref[rows, XZ_H[0]:XZ_H[1]] = h2
        h2_hi, h2_lo = _split_bf16(h2, 2)
        part = _dot(h2_hi, wr_ref[...]) + _dot(h2_lo, wr_ref[...])
        logits = part + pltpu.roll(part, LANES - N_EXPERTS, 1) + pltpu.roll(part, LANES - 2 * N_EXPERTS, 1)
        lane = lax.broadcasted_iota(jnp.int32, logits.shape, 1)
        aff = _softmax_rows(jnp.where(lane < N_EXPERTS, logits, NEG_BIG))
        xz_ref[rows, XZ_A[0]:XZ_A[1]] = aff
        afft_ref[:, rows] = aff.T[:N_EXPERTS, :]


def _outproj(h_fw, h_bw, o, y_n, y_c, x_src, w_out, g_head, g_ffn, w_router_p, layer, tm):
    n = x_src.shape[0]
    row = lambda w_: pl.BlockSpec((tm, w_), lambda i: (i, 0))
    per_layer = lambda a: pl.BlockSpec((None,) + a.shape[1:], lambda i: (layer, 0, 0))
    return pl.pallas_call(
        _outproj_kernel,
        grid=(n // tm,),
        in_specs=[row(V_W), row(V_W), row(V_W), row(NA_W), row(MEM_W), row(D_MODEL), per_layer(w_out),
                  per_layer(g_head), per_layer(g_ffn), per_layer(w_router_p)],
        out_specs=[row(XZ_W), pl.BlockSpec((N_EXPERTS, tm), lambda i: (0, i))],
        out_shape=[jax.ShapeDtypeStruct((n, XZ_W), F32), jax.ShapeDtypeStruct((N_EXPERTS, n), F32)],
        compiler_params=_cparams(1),
        name="out_proj_router",
    )(h_fw, h_bw, o, y_n, y_c, x_src, w_out, g_head, g_ffn, w_router_p)


def _lane_cumsum(mask_f, tri_b):
    n = mask_f.shape[1]
    run = jnp.zeros((mask_f.shape[0], 1), F32)
    parts = []
    for j in range(n // LANES):
        local = _dot(mask_f[:, j * LANES:(j + 1) * LANES].astype(BF16), tri_b) + run
        parts.append(local)
        run = local[:, LANES - 1:LANES]
    return jnp.concatenate(parts, axis=1)


def _select_kernel(aff_ref, idx_ref, csum_ref, cend_ref, *, cap):
    bits = pltpu.bitcast(aff_ref[...], jnp.int32)
    n_e, n_tok = bits.shape
    thr = jnp.zeros((n_e, 1), jnp.int32)
    for bit in range(30, -1, -1):
        cand = thr | (1 << bit)
        cnt = jnp.sum(jnp.where(bits >= cand, 1.0, 0.0), axis=1, keepdims=True)
        thr = jnp.where(cnt >= cap, cand, thr)
    s_i = lax.broadcasted_iota(jnp.int32, (LANES, LANES), 0)
    t_i = lax.broadcasted_iota(jnp.int32, (LANES, LANES), 1)
    tri_b = jnp.where(s_i <= t_i, 1.0, 0.0).astype(BF16)
    gt = bits > thr
    eq_f = jnp.where(bits == thr, 1.0, 0.0)
    need = cap - jnp.sum(jnp.where(gt, 1.0, 0.0), axis=1, keepdims=True)
    eq_rank = _lane_cumsum(eq_f, tri_b)
    sel_f = jnp.where(gt | ((eq_f > 0.5) & (eq_rank <= need)), 1.0, 0.0)
    n_chunks = n_tok // LANES
    assert n_chunks <= LANES and cap <= 16 * 256
    csum_ref[...] = jnp.zeros_like(csum_ref)
    run = jnp.zeros((n_e, 1), F32)
    chunk_lane = lax.broadcasted_iota(jnp.int32, (n_e, LANES), 1)
    cend = jnp.full((n_e, LANES), float(2 * cap), F32)
    for j in range(n_chunks):
        local = _dot(sel_f[:, j * LANES:(j + 1) * LANES].astype(BF16), tri_b) + run
        run = local[:, LANES - 1:LANES]
        cend = jnp.where(chunk_lane == j, run, cend)
        for e in range(n_e):
            csum_ref[e, j:j + 1, :] = local[e:e + 1, :]
    for e in range(n_e):
        cend_ref[e] = cend[e:e + 1, :]
    slot = lax.broadcasted_iota(jnp.int32, (cap, 1), 0).astype(F32)
    lane = lax.broadcasted_iota(jnp.int32, (cap, LANES), 1)

    def per_expert(e, cols):
        counts = csum_ref[e]
        hi = jnp.floor(counts * (1.0 / 16.0))
        lo = counts - 16.0 * hi
        full_chunks = jnp.sum(jnp.where(cend_ref[e] <= slot, 1.0, 0.0), axis=1, keepdims=True)
        pick = jnp.where(lane == full_chunks.astype(jnp.int32), 1.0, 0.0).astype(BF16)
        chunk_counts = 16.0 * _dot(pick, hi.astype(BF16)) + _dot(pick, lo.astype(BF16))
        inside = jnp.sum(jnp.where(chunk_counts <= slot, 1.0, 0.0), axis=1, keepdims=True)
        return jnp.where(lane == e, float(LANES) * full_chunks + inside, cols)

    cols = lax.fori_loop(0, n_e, per_expert, jnp.zeros((cap, LANES), F32))
    idx_ref[...] = cols.T[:n_e, :].astype(jnp.int32) + pl.program_id(0) * n_tok


def _select(aff_t, bsz, seq, cap):
    return pl.pallas_call(
        functools.partial(_select_kernel, cap=cap),
        grid=(bsz,),
        in_specs=[pl.BlockSpec((N_EXPERTS, seq), lambda b: (0, b))],
        out_specs=pl.BlockSpec((None, N_EXPERTS, cap), lambda b: (b, 0, 0)),
        out_shape=jax.ShapeDtypeStruct((bsz, N_EXPERTS, cap), jnp.int32),
        scratch_shapes=[pltpu.VMEM((N_EXPERTS, LANES, LANES), F32), pltpu.VMEM((N_EXPERTS, 1, LANES), F32)],
        compiler_params=_cparams(1),
        name="expert_select",
    )(aff_t)


FF_CHUNK = 512


N_ROWBUF = 3
W_CHUNKS = 8


def _experts_kernel(idx_prev, idx_cur, idx_next, xz_hbm, w1_hbm, w3_hbm, w2_hbm, out_hbm, buf, xb_ref, w1_ref,
                    w3_ref, w2_ref, st1, st3, st2, gsem, ssem, wsem, *, cap, layer, chunks_per_step):
    del xz_hbm
    e, b = pl.program_id(0), pl.program_id(1)
    n_e, n_b = pl.num_programs(0), pl.num_programs(1)
    k = e * n_b + b
    wslot = lax.rem(e, 2)
    up_rows, down_rows = D_MODEL // W_CHUNKS, D_FF // W_CHUNKS

    def weight_chunk_copies(expert, c):
        up = pl.ds(pl.multiple_of(c * up_rows, up_rows), up_rows)
        down = pl.ds(pl.multiple_of(c * down_rows, down_rows), down_rows)
        return (pltpu.make_async_copy(w1_hbm.at[layer, expert, up], st1, wsem.at[0]),
                pltpu.make_async_copy(w3_hbm.at[layer, expert, up], st3, wsem.at[1]),
                pltpu.make_async_copy(w2_hbm.at[layer, expert, down], st2, wsem.at[2]))

    def store_weight_chunk(slot, c):
        up = pl.ds(pl.multiple_of(c * up_rows, up_rows), up_rows)
        down = pl.ds(pl.multiple_of(c * down_rows, down_rows), down_rows)
        w1_ref[slot, up, :] = st1[...].astype(BF16)
        w3_ref[slot, up, :] = st3[...].astype(BF16)
        w2_ref[slot, down, :] = st2[...].astype(BF16)

    @pl.when(k == 0)
    def _():
        def load_chunk(c, carry):
            copies = weight_chunk_copies(0, c)
            for cp in copies:
                cp.start()
            for cp in copies:
                cp.wait()
            store_weight_chunk(0, c)
            return carry
        lax.fori_loop(0, W_CHUNKS, load_chunk, 0)
    slot_cur = lax.rem(k, N_ROWBUF)
    slot_next = lax.rem(k + 1, N_ROWBUF)
    slot_prev = lax.rem(k + 2, N_ROWBUF)

    def fetch_row(idx_ref, slot, i):
        return pltpu.make_async_copy(out_hbm.at[pl.ds(idx_ref[0, i], 1)], buf.at[slot, pl.ds(i, 1)], gsem.at[slot])

    def writeback_row(idx_ref, slot, i):
        return pltpu.make_async_copy(buf.at[slot, pl.ds(i, 1), pl.ds(XZ_X[0], D_MODEL)],
                                     out_hbm.at[pl.ds(idx_ref[0, i], 1), pl.ds(XZ_X[0], D_MODEL)], ssem.at[slot])

    def wait_fetched(slot):
        pltpu.make_async_copy(out_hbm.at[pl.ds(0, cap)], buf.at[slot], gsem.at[slot]).wait()

    def wait_written(slot):
        pltpu.make_async_copy(buf.at[slot, :, pl.ds(XZ_X[0], D_MODEL)],
                              out_hbm.at[pl.ds(0, cap), pl.ds(XZ_X[0], D_MODEL)], ssem.at[slot]).wait()

    def start_all(make_copy):
        def body(i, carry):
            make_copy(i).start()
            return carry
        lax.fori_loop(0, cap, body, 0)

    @pl.when(k == 0)
    def _():
        start_all(lambda i: fetch_row(idx_prev, slot_prev, i))
        wait_fetched(slot_prev)
        start_all(lambda i: fetch_row(idx_cur, slot_cur, i))

    @pl.when(k > 0)
    def _():
        wait_written(slot_next)

    next_chunk = b * chunks_per_step
    stream_on = e + 1 < n_e

    @pl.when(stream_on & (next_chunk < W_CHUNKS))
    def _():
        for cp in weight_chunk_copies(e + 1, next_chunk):
            cp.start()

    wait_fetched(slot_cur)
    lane = lax.broadcasted_iota(jnp.int32, (cap, LANES), 1)
    gate = jnp.sum(jnp.where(lane == e, buf[slot_cur, :, XZ_A[0]:XZ_A[1]], 0.0), axis=1, keepdims=True)
    n_ff = D_FF // FF_CHUNK
    bounds = [0] + [(2 * cap * f) // (n_ff - 1) for f in range(n_ff)]
    xb_ref[...] = buf[slot_cur, :, XZ_H[0]:XZ_H[1]].astype(BF16)
    def start_copies(lo, hi):
        for i in range(lo, hi):
            if i < cap:
                fetch_row(idx_next, slot_next, i).start()
            else:
                writeback_row(idx_prev, slot_prev, i - cap).start()

    for f in range(n_ff):
        xb = xb_ref[...]
        sl = slice(f * FF_CHUNK, (f + 1) * FF_CHUNK)
        start_copies(bounds[f], bounds[f + 1])
        h1 = _dot(xb, w1_ref[wslot, :, sl])
        h3 = _dot(xb, w3_ref[wslot, :, sl])
        hid = (h1 * jax.nn.sigmoid(h1) * h3).astype(BF16)
        buf[slot_cur, :, XZ_X[0]:XZ_X[1]] = (buf[slot_cur, :, XZ_X[0]:XZ_X[1]]
                                             + _dot(hid, w2_ref[wslot, sl, :]) * gate)

    for j in range(chunks_per_step):
        @pl.when(stream_on & (next_chunk + j < W_CHUNKS))
        def _(j=j):
            copies = weight_chunk_copies(e + 1, next_chunk + j)
            if j > 0:
                for cp in copies:
                    cp.start()
            for cp in copies:
                cp.wait()
            store_weight_chunk(1 - wslot, next_chunk + j)

    @pl.when(k == n_e * n_b - 1)
    def _():
        wait_fetched(slot_next)
        wait_written(slot_prev)
        start_all(lambda i: writeback_row(idx_cur, slot_cur, i))
        wait_written(slot_cur)


def _experts(idx, xz, w1, w3, w2, layer, bsz, cap):
    assert bsz >= N_ROWBUF, "rows in flight for neighbouring steps must belong to different sequences"
    idx3 = idx.reshape(bsz * N_EXPERTS, 1, cap)

    def idx_spec(step):
        def index_map(e, b):
            b2 = lax.rem(b + step + bsz, bsz)
            e2 = jnp.clip(e + (b + step + bsz) // bsz - 1, 0, N_EXPERTS - 1)
            return (b2 * N_EXPERTS + e2, 0, 0)
        return pl.BlockSpec((None, 1, cap), index_map, memory_space=pltpu.SMEM)

    any_space = pl.BlockSpec(memory_space=pl.ANY)
    chunks_per_step = -(-W_CHUNKS // bsz)
    return pl.pallas_call(
        functools.partial(_experts_kernel, cap=cap, layer=layer, chunks_per_step=chunks_per_step),
        grid=(N_EXPERTS, bsz),
        in_specs=[idx_spec(-1), idx_spec(0), idx_spec(1), any_space, any_space, any_space, any_space],
        out_specs=any_space,
        out_shape=jax.ShapeDtypeStruct(xz.shape, F32),
        scratch_shapes=[pltpu.VMEM((N_ROWBUF, cap, XZ_W), F32), pltpu.VMEM((cap, D_MODEL), BF16),
                        pltpu.VMEM((2, D_MODEL, D_FF), BF16), pltpu.VMEM((2, D_MODEL, D_FF), BF16),
                        pltpu.VMEM((2, D_FF, D_MODEL), BF16),
                        pltpu.VMEM((D_MODEL // W_CHUNKS, D_FF), F32), pltpu.VMEM((D_MODEL // W_CHUNKS, D_FF), F32),
                        pltpu.VMEM((D_FF // W_CHUNKS, D_MODEL), F32),
                        pltpu.SemaphoreType.DMA((N_ROWBUF,)), pltpu.SemaphoreType.DMA((N_ROWBUF,)),
                        pltpu.SemaphoreType.DMA((3,))],
        input_output_aliases={3: 0},
        compiler_params=_cparams(2),
        name="experts",
    )(idx3, idx3, idx3, xz, w1, w3, w2)


def _repack_w_in(w_in):
    mlstm_w = 2 * M_HEADS * M_DQK + 2 * M_HEADS * M_DV
    pad = jnp.zeros(w_in.shape[:2] + (LANES - N_GATES,), w_in.dtype)
    return jnp.concatenate([w_in[..., :mlstm_w + N_GATES], pad, w_in[..., mlstm_w + N_GATES:]], axis=-1).astype(BF16)


def kernel(x, mem, g_mix, w_in, b_gates, conv_qk, g_mlstm_head, na_gq, na_gk, na_rpb, g_mem, w_mem_kv, mem_gq,
           mem_gk, w_out, g_ffn, w_router, w1, w3, w2):
    bsz, seq, _ = x.shape
    depth = w_in.shape[0]
    rows = seq // GRID_W
    cap = EC_CAPACITY * seq // N_EXPERTS
    tm = 512

    w_in_b = _repack_w_in(w_in)
    w_out_b = w_out.astype(BF16)
    w_kv_b = w_mem_kv.astype(BF16)
    b_gate_p = jnp.pad(b_gates, ((0, 0), (0, LANES - N_GATES)))[:, None, :]
    w_router_p = jnp.pad(jnp.concatenate(_split_bf16(w_router, 3), axis=-1),
                         ((0, 0), (0, 0), (0, LANES - 3 * N_EXPERTS)))
    na_bias = jax.vmap(lambda r: _na_bias_table(r, rows))(na_rpb)
    row3 = lambda a: a[:, None, :]
    per_head = lambda a, n_heads: row3(jnp.tile(a, (1, n_heads)))
    g_mix3, g_mem3, g_head3, g_ffn3 = row3(g_mix), row3(g_mem), row3(g_mlstm_head), row3(g_ffn)
    na_gq3, na_gk3 = per_head(na_gq, NA_HEADS), per_head(na_gk, NA_HEADS)
    mem_gq3, mem_gk3 = per_head(mem_gq, MEM_HEADS), per_head(mem_gk, MEM_HEADS)

    xs = x.reshape(bsz * seq, D_MODEL)
    for l in range(depth):
        q_b, k, v, o, gt, na, cq = _inproj(xs, g_mix3, w_in_b, conv_qk, na_gq3, na_gk3, mem_gq3, l, seq,
                                           tm * IN_PROJ_ROW_GROUPS)
        h_fw, h_bw = _mlstm(q_b, k, v, gt, b_gate_p, l, bsz, seq)
        y_n = _neighbourhood_attention(na, na_bias, l, bsz, seq)
        k_m, v_m = _mem_kv(mem, g_mem3, w_kv_b, mem_gk3, l)
        y_c = _mem_attention(cq, k_m, v_m, bsz, seq, tm)
        xz, aff_t = _outproj(h_fw, h_bw, o, y_n, y_c, xs, w_out_b, g_head3, g_ffn3, w_router_p, l,
                             tm * OUT_PROJ_ROW_GROUPS // 2)
        idx = _select(aff_t, bsz, seq, cap)
        xs = _experts(idx, xz, w1, w3, w2, l, bsz, cap)
    return xs[:, XZ_X[0]:XZ_X[1]].reshape(bsz, seq, D_MODEL)
```
